```python
import jax, jax.numpy as jnp
from jax import lax
import numpy as np


D_MODEL = 1024
BATCH = 4
SEQ = 8192
DEPTH = 4

CHUNK = 64
D_MIX = D_MODEL
D_RWKV = D_MIX // 2
HEAD_DIM = 64
N_HEADS_RWKV = D_RWKV // HEAD_DIM
D_CONV = D_MIX - D_RWKV
CONV_WIDTH = 31
DECAY_LORA = 64
AAA_LORA = 64
MV_LORA = 32
N_SHIFT = 4 * D_RWKV + DECAY_LORA + AAA_LORA
N_IN = N_SHIFT + 3 * D_CONV
RMS_EPS = 1e-6
GN_EPS = 64e-5
LN_EPS = 1e-5

kernel_name = 'hybrid_rwkv7_conformer_conv_trunk'


def rmsnorm(x, g):
    xf = x.astype(jnp.float32)
    y = xf * lax.rsqrt(jnp.mean(xf * xf, axis=-1, keepdims=True) + RMS_EPS)
    return (y * g.astype(jnp.float32)).astype(x.dtype)


def layer_norm(x, g, b):
    xf = x.astype(jnp.float32)
    mu = jnp.mean(xf, axis=-1, keepdims=True)
    var = jnp.mean(jnp.square(xf - mu), axis=-1, keepdims=True)
    y = (xf - mu) * lax.rsqrt(var + LN_EPS) * g.astype(jnp.float32) + b.astype(jnp.float32)
    return y.astype(x.dtype)


def group_norm_heads(o, g, b):
    mu = jnp.mean(o, axis=-1, keepdims=True)
    var = jnp.mean(jnp.square(o - mu), axis=-1, keepdims=True)
    return (o - mu) * lax.rsqrt(var + GN_EPS) * g.astype(jnp.float32) + b.astype(jnp.float32)


def token_shift(p, mu):
    prev = jnp.pad(p, ((0, 0), (1, 0), (0, 0)))[:, :-1]
    return p + (prev - p) * mu


def split_heads(t):
    B, S, _ = t.shape
    return t.reshape(B, S, N_HEADS_RWKV, HEAD_DIM)


def wkv7_scan(r, w, k, v, kk, a):
    B, S, H, N = r.shape

    def step(state, inp):
        r_t, w_t, k_t, v_t, kk_t, a_t = inp
        sa = jnp.einsum('bhvk,bhk->bhv', state, -kk_t)
        state = (state * w_t[:, :, None, :]
                 + sa[..., None] * (kk_t * a_t)[:, :, None, :]
                 + v_t[..., None] * k_t[:, :, None, :])
        y = jnp.einsum('bhvk,bhk->bhv', state, r_t)
        return state, y

    xs = tuple(jnp.moveaxis(t.astype(jnp.float32), 1, 0) for t in (r, w, k, v, kk, a))
    s0 = jnp.zeros((B, H, N, N), jnp.float32)
    _, ys = lax.scan(step, s0, xs)
    return jnp.moveaxis(ys, 0, 1)


def causal_depthwise_conv(x, w, b):
    K, C = w.shape
    out = lax.conv_general_dilated(
        x, w[:, None, :].astype(x.dtype), window_strides=(1,), padding=[(K - 1, 0)],
        dimension_numbers=('NWC', 'WIO', 'NWC'), feature_group_count=C)
    return out + b


def setup_inputs(seed: int = 0) -> dict:
    key = jax.random.key(seed)
    ks = jax.random.split(key, 24)
    f32 = jnp.float32
    nrm = lambda k, shape, s: jax.random.normal(k, shape, f32) * s
    L, Lv = DEPTH, DEPTH - 1
    return {
        'x': nrm(ks[0], (BATCH, SEQ, D_MODEL), 1.0),
        'pre_g': 1.0 + nrm(ks[1], (L, D_MODEL), 0.05),
        'post_g': 1.0 + nrm(ks[2], (L, D_MODEL), 0.05),
        'w_in': nrm(ks[3], (L, D_MODEL, N_IN), D_MODEL ** -0.5),
        'mu_shift': jax.random.uniform(ks[4], (L, N_SHIFT), f32),
        'w0': jax.random.uniform(ks[5], (L, D_RWKV), f32, -5.0, 0.0),
        'w_up': nrm(ks[6], (L, DECAY_LORA, D_RWKV), 0.5 * DECAY_LORA ** -0.5),
        'a0': nrm(ks[7], (L, D_RWKV), 0.1),
        'a_up': nrm(ks[8], (L, AAA_LORA, D_RWKV), AAA_LORA ** -0.5),
        'v0': nrm(ks[9], (Lv, D_RWKV), 0.1),
        'v_down': nrm(ks[10], (Lv, D_RWKV, MV_LORA), D_RWKV ** -0.5),
        'v_up': nrm(ks[11], (Lv, MV_LORA, D_RWKV), MV_LORA ** -0.5),
        'k_k': 0.85 + nrm(ks[12], (L, D_RWKV), 0.05),
        'k_a': 1.0 + nrm(ks[13], (L, D_RWKV), 0.05),
        'r_k': nrm(ks[14], (L, N_HEADS_RWKV, HEAD_DIM), 0.1),
        'gn_g': 1.0 + nrm(ks[15], (L, D_RWKV), 0.05),
        'gn_b': nrm(ks[16], (L, D_RWKV), 0.01),
        'dw_w': nrm(ks[17], (L, CONV_WIDTH, D_CONV), CONV_WIDTH ** -0.5),
        'dw_b': nrm(ks[18], (L, D_CONV), 0.01),
        'ln_g': 1.0 + nrm(ks[19], (L, D_CONV), 0.05),
        'ln_b': nrm(ks[20], (L, D_CONV), 0.01),
        'w_out': nrm(ks[21], (L, D_MIX, D_MODEL), D_MIX ** -0.5),
    }


def reference(x, pre_g, post_g, w_in, mu_shift, w0, w_up, a0, a_up, v0, v_down, v_up,
              k_k, k_a, r_k, gn_g, gn_b, dw_w, dw_b, ln_g, ln_b, w_out):
    B, S, _ = x.shape
    H, N = N_HEADS_RWKV, HEAD_DIM
    v_first = None
    for l in range(DEPTH):
        h = rmsnorm(x, pre_g[l])
        p = h @ w_in[l]

        ps = token_shift(p[..., :N_SHIFT], mu_shift[l])
        r, k, v, z_r = (ps[..., i * D_RWKV:(i + 1) * D_RWKV] for i in range(4))
        wd = ps[..., 4 * D_RWKV:4 * D_RWKV + DECAY_LORA]
        ad = ps[..., 4 * D_RWKV + DECAY_LORA:]

        w_log = -jax.nn.softplus(-(w0[l] + jnp.tanh(wd) @ w_up[l])) - 0.5
        decay = jnp.exp(-jnp.exp(w_log.astype(jnp.float32)))
        a = jax.nn.sigmoid(a0[l] + ad @ a_up[l])
        if l == 0:
            v_first = v
        else:
            nu = jax.nn.sigmoid(v0[l - 1] + (v @ v_down[l - 1]) @ v_up[l - 1])
            v = v + (v_first - v) * nu
        kk = split_heads(k * k_k[l]).astype(jnp.float32)
        kk = kk * lax.rsqrt(jnp.maximum(jnp.sum(kk * kk, axis=-1, keepdims=True), 1e-24))
        k = k * (1.0 + (a - 1.0) * k_a[l])
        rh, kh, vh = split_heads(r), split_heads(k), split_heads(v)
        o = wkv7_scan(rh, split_heads(decay), kh, vh, kk, split_heads(a))
        o = group_norm_heads(o, gn_g[l].reshape(H, N), gn_b[l].reshape(H, N))
        o = o + jnp.sum(rh * kh * r_k[l], axis=-1, keepdims=True).astype(jnp.float32) * vh.astype(jnp.float32)
        y_rwkv = o.reshape(B, S, D_RWKV).astype(x.dtype) * jax.nn.silu(z_r)

        pc = p[..., N_SHIFT:]
        u, gt, z_c = (pc[..., i * D_CONV:(i + 1) * D_CONV] for i in range(3))
        c = causal_depthwise_conv(u * jax.nn.sigmoid(gt), dw_w[l], dw_b[l])
        c = layer_norm(c, ln_g[l], ln_b[l])
        y_conv = jax.nn.silu(c) * jax.nn.silu(z_c)

        out = jnp.concatenate([y_rwkv, y_conv], axis=-1) @ w_out[l]
        x = x + rmsnorm(out, post_g[l])
    return x
```

```python
import functools

import jax
import jax.numpy as jnp
from jax import lax
from jax.experimental import pallas as pl
from jax.experimental.pallas import tpu as pltpu

D_MODEL = 1024
D_RWKV = 512
HEAD_DIM = 64
N_HEADS = D_RWKV // HEAD_DIM
D_CONV = 512
CONV_WIDTH = 31
LORA = 64
MV_LORA = 32
N_A = 4 * D_RWKV + 2 * LORA
N_B = 3 * D_CONV
RMS_EPS = 1e-6
GN_EPS = 64e-5
LN_EPS = 1e-5

CHUNK = 64
ROW_TILE = 256
SEQ_TILE = 256
CONV_HALO = 32
VMEM_LIMIT = 48 * 1024 * 1024

F32 = jnp.float32
BF16 = jnp.bfloat16
HI = lax.Precision.HIGHEST


def _dot(a, b, dims=((1,), (0,)), precision=None):
    return lax.dot_general(a, b, (dims, ((), ())), precision=precision,
                           preferred_element_type=F32)


def _sigmoid(x):
    return 1.0 / (1.0 + jnp.exp(-x))


def _softplus(x):
    return jnp.maximum(x, 0.0) + jnp.log(1.0 + jnp.exp(-jnp.abs(x)))


def _inproj_kernel(x_ref, g_ref, wa_ref, wb_ref, pa_ref, pb_ref):
    x = x_ref[...]
    ms = jnp.mean(x * x, axis=-1, keepdims=True)
    h = (x * lax.rsqrt(ms + RMS_EPS) * g_ref[...]).astype(BF16)
    pa_ref[...] = _dot(h, wa_ref[...])
    pb_ref[...] = _dot(h, wb_ref[...])


def _inproj(x2, g, wa, wb):
    rows = x2.shape[0]
    tm = ROW_TILE
    return pl.pallas_call(
        _inproj_kernel,
        grid=(rows // tm,),
        in_specs=[
            pl.BlockSpec((tm, D_MODEL), lambda i: (i, 0)),
            pl.BlockSpec((1, D_MODEL), lambda i: (0, 0)),
            pl.BlockSpec((D_MODEL, N_A), lambda i: (0, 0)),
            pl.BlockSpec((D_MODEL, N_B), lambda i: (0, 0)),
        ],
        out_specs=[
            pl.BlockSpec((tm, N_A), lambda i: (i, 0)),
            pl.BlockSpec((tm, N_B), lambda i: (i, 0)),
        ],
        out_shape=[
            jax.ShapeDtypeStruct((rows, N_A), F32),
            jax.ShapeDtypeStruct((rows, N_B), F32),
        ],
        compiler_params=pltpu.CompilerParams(
            dimension_semantics=("parallel",), vmem_limit_bytes=VMEM_LIMIT),
        name="inproj",
    )(x2, g, wa, wb)


def _scan_chunk_head(at, rt, bt, kt, bend, kend, v, s0, g_end, strict, incl, eye):
    c = at.shape[0]
    ar = jnp.concatenate([at, rt], axis=0)
    bk = jnp.concatenate([bt, kt], axis=0)
    m = _dot(ar, bk, ((1,), (1,)), HI)
    a_ab = jnp.where(strict, m[:c, :c], 0.0)
    a_ak = jnp.where(strict, m[:c, c:], 0.0)
    a_rb = jnp.where(incl, m[c:, :c], 0.0)
    a_rk = jnp.where(incl, m[c:, c:], 0.0)
    p = a_ab
    t = eye + a_ab
    n_sq = max(c.bit_length() - 2, 0)
    for _ in range(n_sq):
        p = _dot(p, p, precision=HI)
        t = t + _dot(t, p, precision=HI)
    ars = _dot(ar, s0, ((1,), (1,)), HI)
    w = ars[:c] + _dot(a_ak, v, precision=HI)
    u = _dot(t, w, precision=HI)
    y = ars[c:] + _dot(a_rb, u, precision=HI) + _dot(a_rk, v, precision=HI)
    s1 = (s0 * g_end + _dot(u, bend, ((0,), (0,)), HI)
          + _dot(v, kend, ((0,), (0,)), HI))
    return y, s1


def _rwkv_kernel(first, ts, c, *refs):
    refs = list(refs)
    pa_ref = refs.pop(0)
    vf_ref = None if first else refs.pop(0)
    mu_ref, w0_ref, wup_ref, a0_ref, aup_ref = (refs.pop(0) for _ in range(5))
    if not first:
        v0_ref, vdn_ref, vup_ref = (refs.pop(0) for _ in range(3))
    kk_ref, ka_ref, rk_ref, gng_ref, gnb_ref, e_ref = (refs.pop(0) for _ in range(6))
    y_ref = refs.pop(0)
    vfo_ref = refs.pop(0) if first else None
    s_scr, prev_scr, r_s, kn_s, kb_s, k2_s, v_s, lw_s, o_s, bonus_s, gate_s = refs

    t_idx = pl.program_id(1)

    @pl.when(t_idx == 0)
    def _():
        s_scr[...] = jnp.zeros_like(s_scr)
        prev_scr[...] = jnp.zeros_like(prev_scr)

    p = pa_ref[...]
    row = lax.broadcasted_iota(jnp.int32, (ts, 1), 0)
    prev = jnp.where(row == 0, prev_scr[0:1, :], pltpu.roll(p, 1, 0))
    prev_scr[0:1, :] = p[ts - 1:ts, :]
    ps = p + (prev - p) * mu_ref[...]

    r = ps[:, 0:D_RWKV]
    k = ps[:, D_RWKV:2 * D_RWKV]
    v = ps[:, 2 * D_RWKV:3 * D_RWKV]
    z = ps[:, 3 * D_RWKV:4 * D_RWKV]
    wd = ps[:, 4 * D_RWKV:4 * D_RWKV + LORA]
    ad = ps[:, 4 * D_RWKV + LORA:]

    e = e_ref[...]
    w_log = -_softplus(-(w0_ref[...] + _dot(jnp.tanh(wd), wup_ref[...], precision=HI))) - 0.5
    lw_s[...] = -jnp.exp(w_log)
    a = _sigmoid(a0_ref[...] + _dot(ad, aup_ref[...], precision=HI))
    if first:
        vfo_ref[...] = v
    else:
        nu = _sigmoid(v0_ref[...] + _dot(_dot(v, vdn_ref[...], precision=HI), vup_ref[...],
                                          precision=HI))
        v = v + (vf_ref[...] - v) * nu
    kk = k * kk_ref[...]
    ssq = _dot(kk * kk, e, precision=HI)
    kn = kk * lax.rsqrt(jnp.maximum(ssq, 1e-24))
    k2 = k * (1.0 + (a - 1.0) * ka_ref[...])
    r_s[...] = r
    kn_s[...] = kn
    kb_s[...] = kn * a
    k2_s[...] = k2
    v_s[...] = v
    bonus_s[...] = _dot(r * k2 * rk_ref[...], e, precision=HI) * v
    gate_s[...] = z * _sigmoid(z)

    ri = lax.broadcasted_iota(jnp.int32, (c, c), 0)
    ci = lax.broadcasted_iota(jnp.int32, (c, c), 1)
    strict = ri > ci
    incl = ri >= ci
    eye = (ri == ci).astype(F32)
    tri = incl.astype(F32)

    def chunk_body(ic, carry):
        rows = pl.ds(pl.multiple_of(ic * c, c), c)
        lw = lw_s[rows, :]
        cum = _dot(tri, lw, precision=HI)
        tot = cum[c - 1:c, :]
        g_in = jnp.exp(cum)
        g_ex = jnp.exp(cum - lw)
        g_inv = jnp.exp(-cum)
        g_rem = jnp.exp(tot - cum)
        g_end = jnp.exp(tot)
        kn_c = kn_s[rows, :]
        kb_c = kb_s[rows, :]
        k2_c = k2_s[rows, :]
        at = -kn_c * g_ex
        rt = r_s[rows, :] * g_in
        bt = kb_c * g_inv
        kt = k2_c * g_inv
        bend = kb_c * g_rem
        kend = k2_c * g_rem
        v_c = v_s[rows, :]
        for h in range(N_HEADS):
            ln = slice(h * HEAD_DIM, (h + 1) * HEAD_DIM)
            y, s1 = _scan_chunk_head(at[:, ln], rt[:, ln], bt[:, ln], kt[:, ln], bend[:, ln],
                                     kend[:, ln], v_c[:, ln], s_scr[h], g_end[:, ln],
                                     strict, incl, eye)
            s_scr[h] = s1
            o_s[rows, ln] = y
        return carry

    lax.fori_loop(0, ts // c, chunk_body, 0)
    o = o_s[...]
    mean = _dot(o, e, precision=HI) * (1.0 / HEAD_DIM)
    d = o - mean
    var = _dot(d * d, e, precision=HI) * (1.0 / HEAD_DIM)
    on = d * lax.rsqrt(var + GN_EPS) * gng_ref[...] + gnb_ref[...]
    y_ref[...] = ((on + bonus_s[...]) * gate_s[...]).astype(y_ref.dtype)


def _rwkv(pa3, vf3, prm, first):
    b, s, _ = pa3.shape
    ts, c = SEQ_TILE, CHUNK
    row = lambda n: pl.BlockSpec((1, n), lambda i, j: (0, 0))
    mat = lambda m, n: pl.BlockSpec((m, n), lambda i, j: (0, 0))
    seq = lambda n: pl.BlockSpec((None, ts, n), lambda i, j: (i, j, 0))
    in_specs = [seq(N_A)]
    args = [pa3]
    if not first:
        in_specs.append(seq(D_RWKV))
        args.append(vf3)
    in_specs += [row(N_A), row(D_RWKV), mat(LORA, D_RWKV), row(D_RWKV), mat(LORA, D_RWKV)]
    args += [prm["mu"], prm["w0"], prm["w_up"], prm["a0"], prm["a_up"]]
    if not first:
        in_specs += [row(D_RWKV), mat(D_RWKV, MV_LORA), mat(MV_LORA, D_RWKV)]
        args += [prm["v0"], prm["v_down"], prm["v_up"]]
    in_specs += [row(D_RWKV)] * 5 + [mat(D_RWKV, D_RWKV)]
    args += [prm["k_k"], prm["k_a"], prm["r_k"], prm["gn_g"], prm["gn_b"], prm["e"]]
    out_specs = [seq(D_RWKV)]
    out_shape = [jax.ShapeDtypeStruct((b, s, D_RWKV), BF16)]
    if first:
        out_specs.append(seq(D_RWKV))
        out_shape.append(jax.ShapeDtypeStruct((b, s, D_RWKV), F32))
    slab = pltpu.VMEM((ts, D_RWKV), F32)
    scratch = [pltpu.VMEM((N_HEADS, HEAD_DIM, HEAD_DIM), F32), pltpu.VMEM((8, N_A), F32)] + [slab] * 9
    return pl.pallas_call(
        functools.partial(_rwkv_kernel, first, ts, c),
        grid=(b, s // ts),
        in_specs=in_specs,
        out_specs=out_specs,
        out_shape=out_shape,
        scratch_shapes=scratch,
        compiler_params=pltpu.CompilerParams(
            dimension_semantics=("parallel", "arbitrary"), vmem_limit_bytes=VMEM_LIMIT),
        name="rwkv_first" if first else "rwkv",
    )(*args)


def _conv_kernel(ts, pb_ref, w_ref, b_ref, g_ref, beta_ref, y_ref, ext):
    t_idx = pl.program_id(1)

    @pl.when(t_idx == 0)
    def _():
        ext[0:CONV_HALO, :] = jnp.zeros((CONV_HALO, D_CONV), F32)

    u = pb_ref[:, 0:D_CONV]
    gt = pb_ref[:, D_CONV:2 * D_CONV]
    zc = pb_ref[:, 2 * D_CONV:3 * D_CONV]
    ext[CONV_HALO:CONV_HALO + ts, :] = u * _sigmoid(gt)
    base = CONV_HALO - (CONV_WIDTH - 1)
    acc = jnp.zeros((ts, D_CONV), F32) + b_ref[...]
    for j in range(CONV_WIDTH):
        acc = acc + w_ref[j:j + 1, :] * ext[base + j:base + j + ts, :]
    ext[0:CONV_HALO, :] = ext[ts:ts + CONV_HALO, :]
    mean = jnp.mean(acc, axis=-1, keepdims=True)
    d = acc - mean
    var = jnp.mean(d * d, axis=-1, keepdims=True)
    cn = d * lax.rsqrt(var + LN_EPS) * g_ref[...] + beta_ref[...]
    y_ref[...] = ((cn * _sigmoid(cn)) * (zc * _sigmoid(zc))).astype(y_ref.dtype)


def _conv(pb3, prm):
    b, s, _ = pb3.shape
    ts = SEQ_TILE
    row = lambda n: pl.BlockSpec((1, n), lambda i, j: (0, 0))
    return pl.pallas_call(
        functools.partial(_conv_kernel, ts),
        grid=(b, s // ts),
        in_specs=[
            pl.BlockSpec((None, ts, N_B), lambda i, j: (i, j, 0)),
            pl.BlockSpec((CONV_HALO, D_CONV), lambda i, j: (0, 0)),
            row(D_CONV), row(D_CONV), row(D_CONV),
        ],
        out_specs=pl.BlockSpec((None, ts, D_CONV), lambda i, j: (i, j, 0)),
        out_shape=jax.ShapeDtypeStruct((b, s, D_CONV), BF16),
        scratch_shapes=[pltpu.VMEM((ts + CONV_HALO, D_CONV), F32)],
        compiler_params=pltpu.CompilerParams(
            dimension_semantics=("parallel", "arbitrary"), vmem_limit_bytes=VMEM_LIMIT),
        name="conv",
    )(pb3, prm["dw_w"], prm["dw_b"], prm["ln_g"], prm["ln_b"])


def _outproj_kernel(x_ref, yr_ref, yc_ref, wr_ref, wc_ref, g_ref, o_ref):
    out = _dot(yr_ref[...], wr_ref[...]) + _dot(yc_ref[...], wc_ref[...])
    ms = jnp.mean(out * out, axis=-1, keepdims=True)
    o_ref[...] = x_ref[...] + out * lax.rsqrt(ms + RMS_EPS) * g_ref[...]


def _outproj(x2, yr, yc, wr, wc, g):
    rows = x2.shape[0]
    tm = ROW_TILE
    return pl.pallas_call(
        _outproj_kernel,
        grid=(rows // tm,),
        in_specs=[
            pl.BlockSpec((tm, D_MODEL), lambda i: (i, 0)),
            pl.BlockSpec((tm, D_RWKV), lambda i: (i, 0)),
            pl.BlockSpec((tm, D_CONV), lambda i: (i, 0)),
            pl.BlockSpec((D_RWKV, D_MODEL), lambda i: (0, 0)),
            pl.BlockSpec((D_CONV, D_MODEL), lambda i: (0, 0)),
            pl.BlockSpec((1, D_MODEL), lambda i: (0, 0)),
        ],
        out_specs=pl.BlockSpec((tm, D_MODEL), lambda i: (i, 0)),
        out_shape=jax.ShapeDtypeStruct((rows, D_MODEL), F32),
        compiler_params=pltpu.CompilerParams(
            dimension_semantics=("parallel",), vmem_limit_bytes=VMEM_LIMIT),
        name="outproj",
    )(x2, yr, yc, wr, wc, g)


def kernel(x, pre_g, post_g, w_in, mu_shift, w0, w_up, a0, a_up, v0, v_down, v_up, k_k, k_a, r_k,
           gn_g, gn_b, dw_w, dw_b, ln_g, ln_b, w_out):
    b, s, _ = x.shape
    depth = w_in.shape[0]
    assert s % SEQ_TILE == 0 and (b * s) % ROW_TILE == 0 and SEQ_TILE % CHUNK == 0
    head = jnp.arange(D_RWKV, dtype=jnp.int32) // HEAD_DIM
    e = (head[:, None] == head[None, :]).astype(F32)
    w_in_bf = w_in.astype(BF16)
    w_out_bf = w_out.astype(BF16)
    dw_w_pad = jnp.pad(dw_w, ((0, 0), (0, CONV_HALO - CONV_WIDTH), (0, 0)))
    r1 = lambda t: t.reshape(1, -1)

    x2 = x.reshape(b * s, D_MODEL)
    v_first = None
    for l in range(depth):
        pa, pb = _inproj(x2, r1(pre_g[l]), w_in_bf[l, :, :N_A], w_in_bf[l, :, N_A:])
        prm = dict(mu=r1(mu_shift[l]), w0=r1(w0[l]), w_up=w_up[l], a0=r1(a0[l]), a_up=a_up[l],
                   k_k=r1(k_k[l]), k_a=r1(k_a[l]), r_k=r1(r_k[l]), gn_g=r1(gn_g[l]),
                   gn_b=r1(gn_b[l]), e=e, dw_w=dw_w_pad[l], dw_b=r1(dw_b[l]), ln_g=r1(ln_g[l]),
                   ln_b=r1(ln_b[l]))
        pa3 = pa.reshape(b, s, N_A)
        if l == 0:
            yr, v_first = _rwkv(pa3, None, prm, True)
        else:
            prm.update(v0=r1(v0[l - 1]), v_down=v_down[l - 1], v_up=v_up[l - 1])
            yr = _rwkv(pa3, v_first, prm, False)[0]
        yc = _conv(pb.reshape(b, s, N_B), prm)
        x2 = _outproj(x2, yr.reshape(b * s, D_RWKV), yc.reshape(b * s, D_CONV),
                      w_out_bf[l, :D_RWKV], w_out_bf[l, D_RWKV:], r1(post_g[l]))
    return x2.reshape(b, s, D_MODEL)
```

```python
import functools

import jax
import jax.numpy as jnp
from jax import lax
from jax.experimental import pallas as pl
from jax.experimental.pallas import tpu as pltpu

D_MODEL = 1024
D_RWKV = 512
HEAD_DIM = 64
N_HEADS = D_RWKV // HEAD_DIM
D_CONV = 512
CONV_WIDTH = 31
LORA = 64
MV_LORA = 32
N_A = 4 * D_RWKV + 2 * LORA
N_B = 3 * D_CONV
RMS_EPS = 1e-6
GN_EPS = 64e-5
LN_EPS = 1e-5

CHUNK = 64
ROW_TILE = 256
SEQ_TILE = 256
CONV_HALO = 32
VMEM_LIMIT = 48 * 1024 * 1024

F32 = jnp.float32
BF16 = jnp.bfloat16


def _dot(a, b, dims=((1,), (0,))):
    return lax.dot_general(a, b, (dims, ((), ())), preferred_element_type=F32)


def _sigmoid(x):
    return 1.0 / (1.0 + jnp.exp(-x))


def _softplus(x):
    return jnp.maximum(x, 0.0) + jnp.log(1.0 + jnp.exp(-jnp.abs(x)))


def _inproj_kernel(x_ref, g_ref, wa_ref, wb_ref, pa_ref, pb_ref):
    x = x_ref[...]
    ms = jnp.mean(x * x, axis=-1, keepdims=True)
    h = (x * lax.rsqrt(ms + RMS_EPS) * g_ref[...]).astype(BF16)
    pa_ref[...] = _dot(h, wa_ref[...])
    pb_ref[...] = _dot(h, wb_ref[...])


def _inproj(x2, g, wa, wb):
    rows = x2.shape[0]
    tm = ROW_TILE
    return pl.pallas_call(
        _inproj_kernel,
        grid=(rows // tm,),
        in_specs=[
            pl.BlockSpec((tm, D_MODEL), lambda i: (i, 0)),
            pl.BlockSpec((1, D_MODEL), lambda i: (0, 0)),
            pl.BlockSpec((D_MODEL, N_A), lambda i: (0, 0)),
            pl.BlockSpec((D_MODEL, N_B), lambda i: (0, 0)),
        ],
        out_specs=[
            pl.BlockSpec((tm, N_A), lambda i: (i, 0)),
            pl.BlockSpec((tm, N_B), lambda i: (i, 0)),
        ],
        out_shape=[
            jax.ShapeDtypeStruct((rows, N_A), F32),
            jax.ShapeDtypeStruct((rows, N_B), F32),
        ],
        compiler_params=pltpu.CompilerParams(
            dimension_semantics=("parallel",), vmem_limit_bytes=VMEM_LIMIT),
        name="inproj",
    )(x2, g, wa, wb)


def _scan_chunk(ar, bk, bkend, v, s0, g_end, strict, incl, eye):
    heads = range(len(v))
    c = v[0].shape[0]
    m = [_dot(ar[h], bk[h], ((1,), (1,))) for h in heads]
    ars = [_dot(ar[h], s0[h].astype(BF16), ((1,), (1,))) for h in heads]
    a_ab = [jnp.where(strict, m[h][:c, :c], 0.0) for h in heads]
    a_k = [jnp.concatenate([jnp.where(strict, m[h][:c, c:], 0.0),
                            jnp.where(incl, m[h][c:, c:], 0.0)], axis=0).astype(BF16)
           for h in heads]
    a_rb = [jnp.where(incl, m[h][c:, :c], 0.0).astype(BF16) for h in heads]
    p = [a_ab[h].astype(BF16) for h in heads]
    t = [eye + a_ab[h] for h in heads]
    p = [_dot(p[h], p[h]).astype(BF16) for h in heads]
    akv = [_dot(a_k[h], v[h]) for h in heads]
    for _ in range(c.bit_length() - 3):
        pt = [_dot(jnp.concatenate([p[h], t[h].astype(BF16)], axis=0), p[h]) for h in heads]
        t = [t[h] + pt[h][c:] for h in heads]
        p = [pt[h][:c].astype(BF16) for h in heads]
    t = [t[h] + _dot(t[h].astype(BF16), p[h]) for h in heads]
    w = [(ars[h][:c] + akv[h][:c]).astype(BF16) for h in heads]
    u = [_dot(t[h].astype(BF16), w[h]).astype(BF16) for h in heads]
    y = [ars[h][c:] + akv[h][c:] + _dot(a_rb[h], u[h]) for h in heads]
    s1 = [s0[h] * g_end[h] + _dot(jnp.concatenate([u[h], v[h]], axis=0), bkend[h], ((0,), (0,)))
          for h in heads]
    return y, s1


def _rwkv_kernel(first, ts, c, *refs):
    refs = list(refs)
    pa_ref = refs.pop(0)
    vf_ref = None if first else refs.pop(0)
    mu_ref, w0_ref, wup_ref, a0_ref, aup_ref = (refs.pop(0) for _ in range(5))
    if not first:
        v0_ref, vdn_ref, vup_ref = (refs.pop(0) for _ in range(3))
    kk_ref, ka_ref, rk_ref, gng_ref, gnb_ref, e_ref = (refs.pop(0) for _ in range(6))
    y_ref = refs.pop(0)
    vfo_ref = refs.pop(0) if first else None
    s_scr, prev_scr, r_s, kn_s, kb_s, k2_s, v_s, lw_s, o_s, bonus_s, gate_s = refs

    t_idx = pl.program_id(1)

    @pl.when(t_idx == 0)
    def _():
        s_scr[...] = jnp.zeros_like(s_scr)
        prev_scr[...] = jnp.zeros_like(prev_scr)

    p = pa_ref[...]
    row = lax.broadcasted_iota(jnp.int32, (ts, 1), 0)
    prev = jnp.where(row == 0, prev_scr[0:1, :], pltpu.roll(p, 1, 0))
    prev_scr[0:1, :] = p[ts - 1:ts, :]
    ps = p + (prev - p) * mu_ref[...]

    r = ps[:, 0:D_RWKV]
    k = ps[:, D_RWKV:2 * D_RWKV]
    v = ps[:, 2 * D_RWKV:3 * D_RWKV]
    z = ps[:, 3 * D_RWKV:4 * D_RWKV]
    wd = ps[:, 4 * D_RWKV:4 * D_RWKV + LORA]
    ad = ps[:, 4 * D_RWKV + LORA:]

    e = e_ref[...]
    head_sum = lambda t: _dot(t.astype(BF16), e)
    lora = lambda t, w_ref: _dot(t.astype(BF16), w_ref[...].astype(BF16))
    w_log = -_softplus(-(w0_ref[...] + lora(jnp.tanh(wd), wup_ref))) - 0.5
    lw_s[...] = -jnp.exp(w_log)
    a = _sigmoid(a0_ref[...] + lora(ad, aup_ref))
    if first:
        vfo_ref[...] = v
    else:
        nu = _sigmoid(v0_ref[...] + lora(lora(v, vdn_ref), vup_ref))
        v = v + (vf_ref[...] - v) * nu
    kk = k * kk_ref[...]
    kn = kk * lax.rsqrt(jnp.maximum(head_sum(kk * kk), 1e-24))
    k2 = k * (1.0 + (a - 1.0) * ka_ref[...])
    r_s[...] = r
    kn_s[...] = kn
    kb_s[...] = kn * a
    k2_s[...] = k2
    v_s[...] = v
    bonus_s[...] = head_sum(r * k2 * rk_ref[...]) * v
    gate_s[...] = z * _sigmoid(z)

    ri = lax.broadcasted_iota(jnp.int32, (c, c), 0)
    ci = lax.broadcasted_iota(jnp.int32, (c, c), 1)
    strict = ri > ci
    incl = ri >= ci
    eye = (ri == ci).astype(F32)
    tri = incl.astype(BF16)

    def chunk_body(ic, carry):
        rows = pl.ds(pl.multiple_of(ic * c, c), c)
        lw = lw_s[rows, :]
        lw_hi = lw.astype(BF16)
        lw_lo = (lw - lw_hi.astype(F32)).astype(BF16)
        cum = _dot(tri, lw_hi) + _dot(tri, lw_lo)
        tot = cum[c - 1:c, :]
        g_end = jnp.exp(tot)
        g_inv = jnp.exp(-cum)
        g_rem = jnp.exp(tot - cum)
        kn_c = kn_s[rows, :]
        kb_c = kb_s[rows, :]
        k2_c = k2_s[rows, :]
        ar = jnp.concatenate([-kn_c * jnp.exp(cum - lw), r_s[rows, :] * jnp.exp(cum)],
                             axis=0).astype(BF16)
        bk = jnp.concatenate([kb_c * g_inv, k2_c * g_inv], axis=0).astype(BF16)
        bkend = jnp.concatenate([kb_c * g_rem, k2_c * g_rem], axis=0).astype(BF16)
        v_c = v_s[rows, :].astype(BF16)
        lanes = [slice(h * HEAD_DIM, (h + 1) * HEAD_DIM) for h in range(N_HEADS)]
        y, s1 = _scan_chunk([ar[:, ln] for ln in lanes], [bk[:, ln] for ln in lanes],
                            [bkend[:, ln] for ln in lanes], [v_c[:, ln] for ln in lanes],
                            [s_scr[h] for h in range(N_HEADS)], [g_end[:, ln] for ln in lanes],
                            strict, incl, eye)
        for h, ln in enumerate(lanes):
            s_scr[h] = s1[h]
            o_s[rows, ln] = y[h]
        return carry

    lax.fori_loop(0, ts // c, chunk_body, 0)
    o = o_s[...]
    mean = head_sum(o) * (1.0 / HEAD_DIM)
    d = o - mean
    var = head_sum(d * d) * (1.0 / HEAD_DIM)
    on = d * lax.rsqrt(var + GN_EPS) * gng_ref[...] + gnb_ref[...]
    y_ref[...] = ((on + bonus_s[...]) * gate_s[...]).astype(y_ref.dtype)


def _rwkv(pa3, vf3, prm, first):
    b, s, _ = pa3.shape
    ts, c = SEQ_TILE, CHUNK
    row = lambda n: pl.BlockSpec((1, n), lambda i, j: (0, 0))
    mat = lambda m, n: pl.BlockSpec((m, n), lambda i, j: (0, 0))
    seq = lambda n: pl.BlockSpec((None, ts, n), lambda i, j: (i, j, 0))
    in_specs = [seq(N_A)]
    args = [pa3]
    if not first:
        in_specs.append(seq(D_RWKV))
        args.append(vf3)
    in_specs += [row(N_A), row(D_RWKV), mat(LORA, D_RWKV), row(D_RWKV), mat(LORA, D_RWKV)]
    args += [prm["mu"], prm["w0"], prm["w_up"], prm["a0"], prm["a_up"]]
    if not first:
        in_specs += [row(D_RWKV), mat(D_RWKV, MV_LORA), mat(MV_LORA, D_RWKV)]
        args += [prm["v0"], prm["v_down"], prm["v_up"]]
    in_specs += [row(D_RWKV)] * 5 + [mat(D_RWKV, D_RWKV)]
    args += [prm["k_k"], prm["k_a"], prm["r_k"], prm["gn_g"], prm["gn_b"], prm["e"]]
    out_specs = [seq(D_RWKV)]
    out_shape = [jax.ShapeDtypeStruct((b, s, D_RWKV), BF16)]
    if first:
        out_specs.append(seq(D_RWKV))
        out_shape.append(jax.ShapeDtypeStruct((b, s, D_RWKV), F32))
    slab = pltpu.VMEM((ts, D_RWKV), F32)
    scratch = [pltpu.VMEM((N_HEADS, HEAD_DIM, HEAD_DIM), F32), pltpu.VMEM((8, N_A), F32)] + [slab] * 9
    return pl.pallas_call(
        functools.partial(_rwkv_kernel, first, ts, c),
        grid=(b, s // ts),
        in_specs=in_specs,
        out_specs=out_specs,
        out_shape=out_shape,
        scratch_shapes=scratch,
        compiler_params=pltpu.CompilerParams(
            dimension_semantics=("parallel", "arbitrary"), vmem_limit_bytes=VMEM_LIMIT),
        name="rwkv_first" if first else "rwkv",
    )(*args)


def _conv_kernel(ts, pb_ref, w_ref, b_ref, g_ref, beta_ref, y_ref, ext):
    t_idx = pl.program_id(1)

    @pl.when(t_idx == 0)
    def _():
        ext[0:CONV_HALO, :] = jnp.zeros((CONV_HALO, D_CONV), F32)

    u = pb_ref[:, 0:D_CONV]
    gt = pb_ref[:, D_CONV:2 * D_CONV]
    zc = pb_ref[:, 2 * D_CONV:3 * D_CONV]
    ext[CONV_HALO:CONV_HALO + ts, :] = u * _sigmoid(gt)
    base = CONV_HALO - (CONV_WIDTH - 1)
    acc = jnp.zeros((ts, D_CONV), F32) + b_ref[...]
    for j in range(CONV_WIDTH):
        acc = acc + w_ref[j:j + 1, :] * ext[base + j:base + j + ts, :]
    ext[0:CONV_HALO, :] = ext[ts:ts + CONV_HALO, :]
    mean = jnp.mean(acc, axis=-1, keepdims=True)
    d = acc - mean
    var = jnp.mean(d * d, axis=-1, keepdims=True)
    cn = d * lax.rsqrt(var + LN_EPS) * g_ref[...] + beta_ref[...]
    y_ref[...] = ((cn * _sigmoid(cn)) * (zc * _sigmoid(zc))).astype(y_ref.dtype)


def _conv(pb3, prm):
    b, s, _ = pb3.shape
    ts = SEQ_TILE
    row = lambda n: pl.BlockSpec((1, n), lambda i, j: (0, 0))
    return pl.pallas_call(
        functools.partial(_conv_kernel, ts),
        grid=(b, s // ts),
        in_specs=[
            pl.BlockSpec((None, ts, N_B), lambda i, j: (i, j, 0)),
            pl.BlockSpec((CONV_HALO, D_CONV), lambda i, j: (0, 0)),
            row(D_CONV), row(D_CONV), row(D_CONV),
        ],
        out_specs=pl.BlockSpec((None, ts, D_CONV), lambda i, j: (i, j, 0)),
        out_shape=jax.ShapeDtypeStruct((b, s, D_CONV), BF16),
        scratch_shapes=[pltpu.VMEM((ts + CONV_HALO, D_CONV), F32)],
        compiler_params=pltpu.CompilerParams(
            dimension_semantics=("parallel", "arbitrary"), vmem_limit_bytes=VMEM_LIMIT),
        name="conv",
    )(pb3, prm["dw_w"], prm["dw_b"], prm["ln_g"], prm["ln_b"])


def _outproj_kernel(x_ref, yr_ref, yc_ref, wr_ref, wc_ref, g_ref, o_ref):
    out = _dot(yr_ref[...], wr_ref[...]) + _dot(yc_ref[...], wc_ref[...])
    ms = jnp.mean(out * out, axis=-1, keepdims=True)
    o_ref[...] = x_ref[...] + out * lax.rsqrt(ms + RMS_EPS) * g_ref[...]


def _outproj(x2, yr, yc, wr, wc, g):
    rows = x2.shape[0]
    tm = ROW_TILE
    return pl.pallas_call(
        _outproj_kernel,
        grid=(rows // tm,),
        in_specs=[
            pl.BlockSpec((tm, D_MODEL), lambda i: (i, 0)),
            pl.BlockSpec((tm, D_RWKV), lambda i: (i, 0)),
            pl.BlockSpec((tm, D_CONV), lambda i: (i, 0)),
            pl.BlockSpec((D_RWKV, D_MODEL), lambda i: (0, 0)),
            pl.BlockSpec((D_CONV, D_MODEL), lambda i: (0, 0)),
            pl.BlockSpec((1, D_MODEL), lambda i: (0, 0)),
        ],
        out_specs=pl.BlockSpec((tm, D_MODEL), lambda i: (i, 0)),
        out_shape=jax.ShapeDtypeStruct((rows, D_MODEL), F32),
        compiler_params=pltpu.CompilerParams(
            dimension_semantics=("parallel",), vmem_limit_bytes=VMEM_LIMIT),
        name="outproj",
    )(x2, yr, yc, wr, wc, g)


def kernel(x, pre_g, post_g, w_in, mu_shift, w0, w_up, a0, a_up, v0, v_down, v_up, k_k, k_a, r_k,
           gn_g, gn_b, dw_w, dw_b, ln_g, ln_b, w_out):
    b, s, _ = x.shape
    depth = w_in.shape[0]
    assert s % SEQ_TILE == 0 and (b * s) % ROW_TILE == 0 and SEQ_TILE % CHUNK == 0
    head = jnp.arange(D_RWKV, dtype=jnp.int32) // HEAD_DIM
    e = (head[:, None] == head[None, :]).astype(BF16)
    w_in_bf = w_in.astype(BF16)
    w_out_bf = w_out.astype(BF16)
    dw_w_pad = jnp.pad(dw_w, ((0, 0), (0, CONV_HALO - CONV_WIDTH), (0, 0)))
    r1 = lambda t: t.reshape(1, -1)

    x2 = x.reshape(b * s, D_MODEL)
    v_first = None
    for l in range(depth):
        pa, pb = _inproj(x2, r1(pre_g[l]), w_in_bf[l, :, :N_A], w_in_bf[l, :, N_A:])
        prm = dict(mu=r1(mu_shift[l]), w0=r1(w0[l]), w_up=w_up[l], a0=r1(a0[l]), a_up=a_up[l],
                   k_k=r1(k_k[l]), k_a=r1(k_a[l]), r_k=r1(r_k[l]), gn_g=r1(gn_g[l]),
                   gn_b=r1(gn_b[l]), e=e, dw_w=dw_w_pad[l], dw_b=r1(dw_b[l]), ln_g=r1(ln_g[l]),
                   ln_b=r1(ln_b[l]))
        pa3 = pa.reshape(b, s, N_A)
        if l == 0:
            yr, v_first = _rwkv(pa3, None, prm, True)
        else:
            prm.update(v0=r1(v0[l - 1]), v_down=v_down[l - 1], v_up=v_up[l - 1])
            yr = _rwkv(pa3, v_first, prm, False)[0]
        yc = _conv(pb.reshape(b, s, N_B), prm)
        x2 = _outproj(x2, yr.reshape(b * s, D_RWKV), yc.reshape(b * s, D_CONV),
                      w_out_bf[l, :D_RWKV], w_out_bf[l, D_RWKV:], r1(post_g[l]))
    return x2.reshape(b, s, D_MODEL)
```

```python
import functools
import math

import jax
import jax.numpy as jnp
from jax import lax
from jax.experimental import pallas as pl
from jax.experimental.pallas import tpu as pltpu

D_MODEL = 1024
D_RWKV = 512
HEAD_DIM = 64
N_HEADS = D_RWKV // HEAD_DIM
D_CONV = 512
CONV_WIDTH = 31
LORA = 64
MV_LORA = 32
N_A = 4 * D_RWKV + 2 * LORA
N_B = 3 * D_CONV
N_C = 2 * D_CONV
RMS_EPS = 1e-6
GN_EPS = 64e-5
LN_EPS = 1e-5

SUBLANES = 8
CHUNK = 64
CHUNK_GROUP = 2
ROW_TILE = 256
SCAN_TILE = 512
CONV_TILE = 256
CONV_HALO = 32
VMEM_LIMIT = 48 * 1024 * 1024

F32 = jnp.float32
BF16 = jnp.bfloat16


def _dot(a, b, dims=((1,), (0,))):
    return lax.dot_general(a, b, (dims, ((), ())), preferred_element_type=F32)


def _sigmoid(x):
    return 1.0 / (1.0 + jnp.exp(-x))


def _inproj_kernel(tiles_per_seq, x_ref, g_ref, wa_ref, wb_ref, mu_ref, pa_ref, pc_ref, carry):
    tm = x_ref.shape[0]

    @pl.when(pl.program_id(0) % tiles_per_seq == 0)
    def _():
        carry[...] = jnp.zeros_like(carry)

    x = x_ref[...]
    ms = jnp.mean(x * x, axis=-1, keepdims=True)
    h = (x * lax.rsqrt(ms + RMS_EPS) * g_ref[...]).astype(BF16)
    pa = _dot(h, wa_ref[...])
    row = lax.broadcasted_iota(jnp.int32, (tm, 1), 0)
    prev = jnp.where(row == 0, carry[0:1, :], pltpu.roll(pa, 1, 0))
    carry[0:1, :] = pa[tm - 1:tm, :]
    ps = pa + (prev - pa) * mu_ref[...]
    z = ps[:, 3 * D_RWKV:4 * D_RWKV]
    pa_ref[:, 0:3 * D_RWKV] = ps[:, 0:3 * D_RWKV]
    pa_ref[:, 3 * D_RWKV:4 * D_RWKV] = z * _sigmoid(z)
    pa_ref[:, 4 * D_RWKV:] = ps[:, 4 * D_RWKV:]
    pb = _dot(h, wb_ref[...])
    zc = pb[:, 2 * D_CONV:]
    pc_ref[:, 0:D_CONV] = pb[:, 0:D_CONV] * _sigmoid(pb[:, D_CONV:2 * D_CONV])
    pc_ref[:, D_CONV:] = zc * _sigmoid(zc)


def _inproj(x2, g, wa, wb, mu, seq_len):
    rows = x2.shape[0]
    tm = ROW_TILE
    return pl.pallas_call(
        functools.partial(_inproj_kernel, seq_len // tm),
        grid=(rows // tm,),
        in_specs=[
            pl.BlockSpec((tm, D_MODEL), lambda i: (i, 0)),
            pl.BlockSpec((1, D_MODEL), lambda i: (0, 0)),
            pl.BlockSpec((D_MODEL, N_A), lambda i: (0, 0)),
            pl.BlockSpec((D_MODEL, N_B), lambda i: (0, 0)),
            pl.BlockSpec((1, N_A), lambda i: (0, 0)),
        ],
        out_specs=[
            pl.BlockSpec((tm, N_A), lambda i: (i, 0)),
            pl.BlockSpec((tm, N_C), lambda i: (i, 0)),
        ],
        out_shape=[
            jax.ShapeDtypeStruct((rows, N_A), F32),
            jax.ShapeDtypeStruct((rows, N_C), F32),
        ],
        scratch_shapes=[pltpu.VMEM((SUBLANES, N_A), F32)],
        compiler_params=pltpu.CompilerParams(
            dimension_semantics=("arbitrary",), vmem_limit_bytes=VMEM_LIMIT),
        name="inproj",
    )(x2, g, wa, wb, mu)


def _scan_stages_a(st, strict, incl, eye):
    heads = range(len(st["v"]))
    c = st["v"][0].shape[0]

    def gram():
        st["m"] = [_dot(st["ar"][h], st["bk"][h], ((1,), (1,))) for h in heads]

    def square():
        m = st.pop("m")
        a_ab = [jnp.where(strict, m[h][:c, :c], 0.0) for h in heads]
        a_k = [jnp.concatenate([jnp.where(strict, m[h][:c, c:], 0.0),
                                jnp.where(incl, m[h][c:, c:], 0.0)], axis=0).astype(BF16)
               for h in heads]
        st["a_rb"] = [jnp.where(incl, m[h][c:, :c], 0.0).astype(BF16) for h in heads]
        p = [a_ab[h].astype(BF16) for h in heads]
        st["t"] = [eye + a_ab[h] for h in heads]
        st["p"] = [_dot(p[h], p[h]).astype(BF16) for h in heads]
        st["akv"] = [_dot(a_k[h], st["v"][h]) for h in heads]

    def double():
        p, t = st["p"], st["t"]
        pt = [_dot(jnp.concatenate([p[h], t[h].astype(BF16)], axis=0), p[h]) for h in heads]
        st["t"] = [t[h] + pt[h][c:] for h in heads]
        st["p"] = [pt[h][:c].astype(BF16) for h in heads]

    def last():
        p, t = st.pop("p"), st["t"]
        st["t"] = [(t[h] + _dot(t[h].astype(BF16), p[h])).astype(BF16) for h in heads]

    def apply_t():
        t, akv = st.pop("t"), st.pop("akv")
        st["at2"] = [_dot(t[h], st["ar"][h][:c]).astype(BF16) for h in heads]
        st["ua"] = [_dot(t[h], akv[h][:c].astype(BF16)) for h in heads]
        st["yv"] = [akv[h][c:] for h in heads]

    return [gram, square] + [double] * (c.bit_length() - 3) + [last, apply_t]


def _scan_stages_b(st, lo, s_scr, o_s, row0):
    heads = range(s_scr.shape[0])
    c = st["v"][0].shape[0]
    s0, z = {}, {}

    def read_state():
        for h in heads:
            s0[h] = s_scr[h]
            lhs = jnp.concatenate([st["at2"][lo + h], st["ar"][lo + h][c:]], axis=0)
            z[h] = _dot(lhs, s0[h].astype(BF16), ((1,), (1,)))

    def write_state():
        u = [(z[h][:c] + st["ua"][lo + h]).astype(BF16) for h in heads]
        y = [z[h][c:] + st["yv"][lo + h] + _dot(st["a_rb"][lo + h], u[h]) for h in heads]
        for h in heads:
            uv = jnp.concatenate([u[h], st["v"][lo + h]], axis=0)
            s_scr[h] = s0[h] * st["g_end"][lo + h] + _dot(uv, st["bkend"][lo + h], ((0,), (0,)))
            o_s[row0:row0 + c, h * HEAD_DIM:(h + 1) * HEAD_DIM] = y[h]

    return [read_state, write_state]


def _rwkv_kernel(first, ts, c, *refs):
    refs = list(refs)
    pa_ref = refs.pop(0)
    vf_ref = None if first else refs.pop(0)
    w0_ref, wup_ref, a0_ref, aup_ref = (refs.pop(0) for _ in range(4))
    if not first:
        v0_ref, vdn_ref, vup_ref = (refs.pop(0) for _ in range(3))
    kk_ref, ka_ref, rk_ref, gng_ref, gnb_ref, e_ref = (refs.pop(0) for _ in range(6))
    y_ref = refs.pop(0)
    vfo_ref = refs.pop(0) if first else None
    s_scr, r_s, kn_s, kb_s, k2_s, v_s, lw_s, o_s, bonus_s = refs

    @pl.when(pl.program_id(1) == 0)
    def _():
        s_scr[...] = jnp.zeros_like(s_scr)

    r = pa_ref[:, 0:D_RWKV]
    k = pa_ref[:, D_RWKV:2 * D_RWKV]
    v = pa_ref[:, 2 * D_RWKV:3 * D_RWKV]
    wd = pa_ref[:, 4 * D_RWKV:4 * D_RWKV + LORA]
    ad = pa_ref[:, 4 * D_RWKV + LORA:]

    e = e_ref[...]
    head_sum = lambda t: _dot(t.astype(BF16), e)
    lora = lambda t, w_ref: _dot(t.astype(BF16), w_ref[...].astype(BF16))
    lw_s[...] = -math.exp(-0.5) * _sigmoid(w0_ref[...] + lora(jnp.tanh(wd), wup_ref))
    a = _sigmoid(a0_ref[...] + lora(ad, aup_ref))
    if first:
        vfo_ref[...] = v
    else:
        nu = _sigmoid(v0_ref[...] + lora(lora(v, vdn_ref), vup_ref))
        v = v + (vf_ref[...] - v) * nu
    kk = k * kk_ref[...]
    kn = kk * lax.rsqrt(jnp.maximum(head_sum(kk * kk), 1e-24))
    k2 = k * (1.0 + (a - 1.0) * ka_ref[...])
    r_s[...] = r
    kn_s[...] = kn
    kb_s[...] = kn * a
    k2_s[...] = k2
    v_s[...] = v.astype(BF16)
    bonus_s[...] = head_sum(r * k2 * rk_ref[...]) * v

    ri = lax.broadcasted_iota(jnp.int32, (c, c), 0)
    ci = lax.broadcasted_iota(jnp.int32, (c, c), 1)
    strict = ri > ci
    incl = ri >= ci
    eye = (ri == ci).astype(F32)
    tri = incl.astype(BF16)
    lanes = [slice(h * HEAD_DIM, (h + 1) * HEAD_DIM) for h in range(N_HEADS)]

    def prepare(ic):
        rows = slice(ic * c, (ic + 1) * c)
        lw = lw_s[rows, :]
        lw_hi = lw.astype(BF16)
        lw_lo = (lw - lw_hi.astype(F32)).astype(BF16)
        cum = _dot(tri, lw_hi) + _dot(tri, lw_lo)
        tot = cum[c - 1:c, :]
        g_end = jnp.exp(tot)
        g_inv = jnp.exp(-cum)
        g_rem = jnp.exp(tot - cum)
        kn_c = kn_s[rows, :]
        kb_c = kb_s[rows, :]
        k2_c = k2_s[rows, :]
        ar = jnp.concatenate([-kn_c * jnp.exp(cum - lw), r_s[rows, :] * jnp.exp(cum)],
                             axis=0).astype(BF16)
        bk = jnp.concatenate([kb_c * g_inv, k2_c * g_inv], axis=0).astype(BF16)
        bkend = jnp.concatenate([kb_c * g_rem, k2_c * g_rem], axis=0).astype(BF16)
        v_c = v_s[rows, :]
        return dict(ar=[ar[:, ln] for ln in lanes], bk=[bk[:, ln] for ln in lanes],
                    bkend=[bkend[:, ln] for ln in lanes], v=[v_c[:, ln] for ln in lanes],
                    g_end=[g_end[:, ln] for ln in lanes])

    def prepare_group(ig):
        parts = [prepare(ig * CHUNK_GROUP + j) for j in range(CHUNK_GROUP)]
        return {key: sum((part[key] for part in parts), []) for key in parts[0]}

    n_groups = ts // (c * CHUNK_GROUP)
    st = prepare_group(0)
    for stage in _scan_stages_a(st, strict, incl, eye):
        stage()
    for ig in range(n_groups):
        b_stages = []
        for j in range(CHUNK_GROUP):
            b_stages += _scan_stages_b(st, j * N_HEADS, s_scr, o_s, (ig * CHUNK_GROUP + j) * c)
        if ig + 1 < n_groups:
            nxt = prepare_group(ig + 1)
            a_stages = _scan_stages_a(nxt, strict, incl, eye)
        else:
            nxt, a_stages = None, []
        per_b = -(-len(a_stages) // len(b_stages))
        for i, b_stage in enumerate(b_stages):
            for stage in a_stages[i * per_b:(i + 1) * per_b]:
                stage()
            b_stage()
        st = nxt

    o = o_s[...]
    mean = head_sum(o) * (1.0 / HEAD_DIM)
    d = o - mean
    var = head_sum(d * d) * (1.0 / HEAD_DIM)
    on = d * lax.rsqrt(var + GN_EPS) * gng_ref[...] + gnb_ref[...]
    gate = pa_ref[:, 3 * D_RWKV:4 * D_RWKV]
    y_ref[...] = ((on + bonus_s[...]) * gate).astype(y_ref.dtype)


def _rwkv(pa3, vf3, prm, first):
    b, s, _ = pa3.shape
    ts, c = SCAN_TILE, CHUNK
    row = lambda n: pl.BlockSpec((1, n), lambda i, j: (0, 0))
    mat = lambda m, n: pl.BlockSpec((m, n), lambda i, j: (0, 0))
    seq = lambda n: pl.BlockSpec((None, ts, n), lambda i, j: (i, j, 0))
    in_specs = [seq(N_A)]
    args = [pa3]
    if not first:
        in_specs.append(seq(D_RWKV))
        args.append(vf3)
    in_specs += [row(D_RWKV), mat(LORA, D_RWKV), row(D_RWKV), mat(LORA, D_RWKV)]
    args += [prm["w0"], prm["w_up"], prm["a0"], prm["a_up"]]
    if not first:
        in_specs += [row(D_RWKV), mat(D_RWKV, MV_LORA), mat(MV_LORA, D_RWKV)]
        args += [prm["v0"], prm["v_down"], prm["v_up"]]
    in_specs += [row(D_RWKV)] * 5 + [mat(D_RWKV, D_RWKV)]
    args += [prm["k_k"], prm["k_a"], prm["r_k"], prm["gn_g"], prm["gn_b"], prm["e"]]
    out_specs = [seq(D_RWKV)]
    out_shape = [jax.ShapeDtypeStruct((b, s, D_RWKV), BF16)]
    if first:
        out_specs.append(seq(D_RWKV))
        out_shape.append(jax.ShapeDtypeStruct((b, s, D_RWKV), F32))
    slab = pltpu.VMEM((ts, D_RWKV), F32)
    scratch = ([pltpu.VMEM((N_HEADS, HEAD_DIM, HEAD_DIM), F32)] + [slab] * 4
               + [pltpu.VMEM((ts, D_RWKV), BF16)] + [slab] * 3)
    return pl.pallas_call(
        functools.partial(_rwkv_kernel, first, ts, c),
        grid=(b, s // ts),
        in_specs=in_specs,
        out_specs=out_specs,
        out_shape=out_shape,
        scratch_shapes=scratch,
        compiler_params=pltpu.CompilerParams(
            dimension_semantics=("parallel", "arbitrary"), vmem_limit_bytes=VMEM_LIMIT),
        name="rwkv_first" if first else "rwkv",
    )(*args)


def _conv_kernel(ts, pc_ref, w_ref, b_ref, g_ref, beta_ref, y_ref, ext, shifted):
    @pl.when(pl.program_id(1) == 0)
    def _():
        ext[0:CONV_HALO, :] = jnp.zeros((CONV_HALO, D_CONV), F32)

    ext[CONV_HALO:CONV_HALO + ts, :] = pc_ref[:, 0:D_CONV]
    n_sh = shifted.shape[1]
    for r in range(1, SUBLANES):
        shifted[r - 1] = ext[r:r + n_sh, :]
    base = CONV_HALO - (CONV_WIDTH - 1)
    acc = jnp.zeros((ts, D_CONV), F32) + b_ref[...]
    for j in range(CONV_WIDTH):
        q, r = divmod(base + j, SUBLANES)
        src = ext if r == 0 else shifted.at[r - 1]
        acc = acc + w_ref[j:j + 1, :] * src[q * SUBLANES:q * SUBLANES + ts, :]
    ext[0:CONV_HALO, :] = ext[ts:ts + CONV_HALO, :]
    mean = jnp.mean(acc, axis=-1, keepdims=True)
    d = acc - mean
    var = jnp.mean(d * d, axis=-1, keepdims=True)
    cn = d * lax.rsqrt(var + LN_EPS) * g_ref[...] + beta_ref[...]
    y_ref[...] = ((cn * _sigmoid(cn)) * pc_ref[:, D_CONV:]).astype(y_ref.dtype)


def _conv(pc3, prm):
    b, s, _ = pc3.shape
    ts = CONV_TILE
    row = lambda n: pl.BlockSpec((1, n), lambda i, j: (0, 0))
    return pl.pallas_call(
        functools.partial(_conv_kernel, ts),
        grid=(b, s // ts),
        in_specs=[
            pl.BlockSpec((None, ts, N_C), lambda i, j: (i, j, 0)),
            pl.BlockSpec((CONV_HALO, D_CONV), lambda i, j: (0, 0)),
            row(D_CONV), row(D_CONV), row(D_CONV),
        ],
        out_specs=pl.BlockSpec((None, ts, D_CONV), lambda i, j: (i, j, 0)),
        out_shape=jax.ShapeDtypeStruct((b, s, D_CONV), BF16),
        scratch_shapes=[
            pltpu.VMEM((ts + CONV_HALO, D_CONV), F32),
            pltpu.VMEM((SUBLANES - 1, ts + CONV_HALO - SUBLANES, D_CONV), F32),
        ],
        compiler_params=pltpu.CompilerParams(
            dimension_semantics=("parallel", "arbitrary"), vmem_limit_bytes=VMEM_LIMIT),
        name="conv",
    )(pc3, prm["dw_w"], prm["dw_b"], prm["ln_g"], prm["ln_b"])


def _outproj_kernel(x_ref, yr_ref, yc_ref, wr_ref, wc_ref, g_ref, o_ref):
    out = _dot(yr_ref[...], wr_ref[...]) + _dot(yc_ref[...], wc_ref[...])
    ms = jnp.mean(out * out, axis=-1, keepdims=True)
    o_ref[...] = x_ref[...] + out * lax.rsqrt(ms + RMS_EPS) * g_ref[...]


def _outproj(x2, yr, yc, wr, wc, g):
    rows = x2.shape[0]
    tm = ROW_TILE
    return pl.pallas_call(
        _outproj_kernel,
        grid=(rows // tm,),
        in_specs=[
            pl.BlockSpec((tm, D_MODEL), lambda i: (i, 0)),
            pl.BlockSpec((tm, D_RWKV), lambda i: (i, 0)),
            pl.BlockSpec((tm, D_CONV), lambda i: (i, 0)),
            pl.BlockSpec((D_RWKV, D_MODEL), lambda i: (0, 0)),
            pl.BlockSpec((D_CONV, D_MODEL), lambda i: (0, 0)),
            pl.BlockSpec((1, D_MODEL), lambda i: (0, 0)),
        ],
        out_specs=pl.BlockSpec((tm, D_MODEL), lambda i: (i, 0)),
        out_shape=jax.ShapeDtypeStruct((rows, D_MODEL), F32),
        compiler_params=pltpu.CompilerParams(
            dimension_semantics=("parallel",), vmem_limit_bytes=VMEM_LIMIT),
        name="outproj",
    )(x2, yr, yc, wr, wc, g)


def kernel(x, pre_g, post_g, w_in, mu_shift, w0, w_up, a0, a_up, v0, v_down, v_up, k_k, k_a, r_k,
           gn_g, gn_b, dw_w, dw_b, ln_g, ln_b, w_out):
    b, s, _ = x.shape
    depth = w_in.shape[0]
    assert s % SCAN_TILE == 0 and s % CONV_TILE == 0 and s % ROW_TILE == 0
    assert SCAN_TILE % (CHUNK * CHUNK_GROUP) == 0 and CHUNK >= 8
    assert CONV_TILE >= CONV_HALO >= CONV_WIDTH - 1
    head = jnp.arange(D_RWKV, dtype=jnp.int32) // HEAD_DIM
    e = (head[:, None] == head[None, :]).astype(BF16)
    w_in_bf = w_in.astype(BF16)
    w_out_bf = w_out.astype(BF16)
    dw_w_pad = jnp.pad(dw_w, ((0, 0), (0, CONV_HALO - CONV_WIDTH), (0, 0)))
    r1 = lambda t: t.reshape(1, -1)

    x2 = x.reshape(b * s, D_MODEL)
    v_first = None
    for l in range(depth):
        pa, pc = _inproj(x2, r1(pre_g[l]), w_in_bf[l, :, :N_A], w_in_bf[l, :, N_A:],
                         r1(mu_shift[l]), s)
        prm = dict(w0=r1(w0[l]), w_up=w_up[l], a0=r1(a0[l]), a_up=a_up[l],
                   k_k=r1(k_k[l]), k_a=r1(k_a[l]), r_k=r1(r_k[l]), gn_g=r1(gn_g[l]),
                   gn_b=r1(gn_b[l]), e=e, dw_w=dw_w_pad[l], dw_b=r1(dw_b[l]), ln_g=r1(ln_g[l]),
                   ln_b=r1(ln_b[l]))
        pa3 = pa.reshape(b, s, N_A)
        if l == 0:
            yr, v_first = _rwkv(pa3, None, prm, True)
        else:
            prm.update(v0=r1(v0[l - 1]), v_down=v_down[l - 1], v_up=v_up[l - 1])
            yr = _rwkv(pa3, v_first, prm, False)[0]
        yc = _conv(pc.reshape(b, s, N_C), prm)
        x2 = _outproj(x2, yr.reshape(b * s, D_RWKV), yc.reshape(b * s, D_CONV),
                      w_out_bf[l, :D_RWKV], w_out_bf[l, D_RWKV:], r1(post_g[l]))
    return x2.reshape(b, s, D_MODEL)
```

```python
import functools
import math

import jax
import jax.numpy as jnp
from jax import lax
from jax.experimental import pallas as pl
from jax.experimental.pallas import tpu as pltpu

D_MODEL = 1024
D_RWKV = 512
HEAD_DIM = 64
N_HEADS = D_RWKV // HEAD_DIM
D_CONV = 512
CONV_WIDTH = 31
LORA = 64
MV_LORA = 32
N_A = 4 * D_RWKV + 2 * LORA
N_B = 3 * D_CONV
N_C = 2 * D_CONV
RMS_EPS = 1e-6
GN_EPS = 64e-5
LN_EPS = 1e-5

SUBLANES = 8
LANES = 128
CHUNK = 64
CHUNK_GROUP = 2
ROW_TILE = 512
OUT_TILE = 1024
SCAN_TILE = 512
CONV_TILE = 256
CONV_HALO = 32
VMEM_LIMIT = 48 * 1024 * 1024

F32 = jnp.float32
BF16 = jnp.bfloat16


def _dot(a, b, dims=((1,), (0,))):
    return lax.dot_general(a, b, (dims, ((), ())), preferred_element_type=F32)


def _sigmoid(x):
    return 1.0 / (1.0 + jnp.exp(-x))


def _inproj_kernel(tiles_per_seq, x_ref, g_ref, wa_ref, wb_ref, mu_ref, pa_ref, pc_ref, carry):
    tm = x_ref.shape[0]

    @pl.when(pl.program_id(0) % tiles_per_seq == 0)
    def _():
        carry[...] = jnp.zeros_like(carry)

    x = x_ref[...]
    ms = jnp.mean(x * x, axis=-1, keepdims=True)
    h = (x * lax.rsqrt(ms + RMS_EPS) * g_ref[...]).astype(BF16)
    pa = _dot(h, wa_ref[...])
    row = lax.broadcasted_iota(jnp.int32, (tm, 1), 0)
    prev = jnp.where(row == 0, carry[0:1, :], pltpu.roll(pa, 1, 0))
    carry[0:1, :] = pa[tm - 1:tm, :]
    ps = pa + (prev - pa) * mu_ref[...]
    z = ps[:, 3 * D_RWKV:4 * D_RWKV]
    pa_ref[:, 0:3 * D_RWKV] = ps[:, 0:3 * D_RWKV]
    pa_ref[:, 3 * D_RWKV:4 * D_RWKV] = z * _sigmoid(z)
    pa_ref[:, 4 * D_RWKV:] = ps[:, 4 * D_RWKV:]
    pb = _dot(h, wb_ref[...])
    zc = pb[:, 2 * D_CONV:]
    pc_ref[:, 0:D_CONV] = pb[:, 0:D_CONV] * _sigmoid(pb[:, D_CONV:2 * D_CONV])
    pc_ref[:, D_CONV:] = zc * _sigmoid(zc)


def _inproj(x2, g, wa, wb, mu, seq_len):
    rows = x2.shape[0]
    tm = ROW_TILE
    return pl.pallas_call(
        functools.partial(_inproj_kernel, seq_len // tm),
        grid=(rows // tm,),
        in_specs=[
            pl.BlockSpec((tm, D_MODEL), lambda i: (i, 0)),
            pl.BlockSpec((1, D_MODEL), lambda i: (0, 0)),
            pl.BlockSpec((D_MODEL, N_A), lambda i: (0, 0)),
            pl.BlockSpec((D_MODEL, N_B), lambda i: (0, 0)),
            pl.BlockSpec((1, N_A), lambda i: (0, 0)),
        ],
        out_specs=[
            pl.BlockSpec((tm, N_A), lambda i: (i, 0)),
            pl.BlockSpec((tm, N_C), lambda i: (i, 0)),
        ],
        out_shape=[
            jax.ShapeDtypeStruct((rows, N_A), F32),
            jax.ShapeDtypeStruct((rows, N_C), F32),
        ],
        scratch_shapes=[pltpu.VMEM((SUBLANES, N_A), F32)],
        compiler_params=pltpu.CompilerParams(
            dimension_semantics=("arbitrary",), vmem_limit_bytes=VMEM_LIMIT),
        name="inproj",
    )(x2, g, wa, wb, mu)


def _interleave(streams):
    tagged = []
    for s, stages in enumerate(streams):
        tagged += [((k + 0.5) / len(stages), s, stage) for k, stage in enumerate(stages)]
    for _, _, stage in sorted(tagged, key=lambda entry: entry[:2]):
        stage()


def _scan_stages_a(st, strict, incl, eye):
    heads = range(len(st["v"]))
    c = st["v"][0].shape[0]

    def gram():
        st["m"] = [_dot(st["ar"][h], st["bk"][h], ((1,), (1,))) for h in heads]

    def square():
        m = st.pop("m")
        a_ab = [jnp.where(strict, m[h][:c, :c], 0.0) for h in heads]
        a_k = [jnp.concatenate([jnp.where(strict, m[h][:c, c:], 0.0),
                                jnp.where(incl, m[h][c:, c:], 0.0)], axis=0).astype(BF16)
               for h in heads]
        st["a_rb"] = [jnp.where(incl, m[h][c:, :c], 0.0).astype(BF16) for h in heads]
        p = [a_ab[h].astype(BF16) for h in heads]
        st["t"] = [eye + a_ab[h] for h in heads]
        st["p"] = [_dot(p[h], p[h]).astype(BF16) for h in heads]
        st["akv"] = [_dot(a_k[h], st["v"][h]) for h in heads]

    def double():
        p, t = st["p"], st["t"]
        pt = [_dot(jnp.concatenate([p[h], t[h].astype(BF16)], axis=0), p[h]) for h in heads]
        st["t"] = [t[h] + pt[h][c:] for h in heads]
        st["p"] = [pt[h][:c].astype(BF16) for h in heads]

    def last():
        p, t = st.pop("p"), st["t"]
        st["t"] = [(t[h] + _dot(t[h].astype(BF16), p[h])).astype(BF16) for h in heads]

    def apply_t():
        t, akv = st.pop("t"), st.pop("akv")
        st["at2"] = [_dot(t[h], st["ar"][h][:c]).astype(BF16) for h in heads]
        st["ua"] = [_dot(t[h], akv[h][:c].astype(BF16)) for h in heads]
        st["yv"] = [akv[h][c:] for h in heads]

    return [gram, square] + [double] * (c.bit_length() - 3) + [last, apply_t]


def _scan_stages_b(st, lo, s_scr, o_s, row0):
    heads = range(s_scr.shape[0])
    c = st["v"][0].shape[0]
    s0, z = {}, {}

    def read_state():
        for h in heads:
            s0[h] = s_scr[h]
            lhs = jnp.concatenate([st["at2"][lo + h], st["ar"][lo + h][c:]], axis=0)
            z[h] = _dot(lhs, s0[h].astype(BF16), ((1,), (1,)))

    def write_state():
        u = [(z[h][:c] + st["ua"][lo + h]).astype(BF16) for h in heads]
        y = [z[h][c:] + st["yv"][lo + h] + _dot(st["a_rb"][lo + h], u[h]) for h in heads]
        for h in heads:
            uv = jnp.concatenate([u[h], st["v"][lo + h]], axis=0)
            s_scr[h] = s0[h] * st["g_end"][lo + h] + _dot(uv, st["bkend"][lo + h], ((0,), (0,)))
            o_s[row0:row0 + c, h * HEAD_DIM:(h + 1) * HEAD_DIM] = y[h]

    return [read_state, write_state]


def _rwkv_kernel(first, ts, c, *refs):
    refs = list(refs)
    pa_ref = refs.pop(0)
    vf_ref = None if first else refs.pop(0)
    w0_ref, wup_ref, a0_ref, aup_ref = (refs.pop(0) for _ in range(4))
    if not first:
        v0_ref, vdn_ref, vup_ref = (refs.pop(0) for _ in range(3))
    kk_ref, ka_ref, rk_ref, gng_ref, gnb_ref, e_ref = (refs.pop(0) for _ in range(6))
    y_ref = refs.pop(0)
    vfo_ref = refs.pop(0) if first else None
    s_scr, o_s, bonus_s = refs

    @pl.when(pl.program_id(1) == 0)
    def _():
        s_scr[...] = jnp.zeros_like(s_scr)

    e = e_ref[...]

    def head_sums(*terms):
        n = terms[0].shape[0]
        tiles = [t[:, lo:lo + LANES].astype(BF16) for t in terms for lo in range(0, D_RWKV, LANES)]
        sums = _dot(jnp.concatenate(tiles, axis=0), e)
        per = D_RWKV // LANES
        return [jnp.concatenate([sums[(i * per + j) * n:(i * per + j + 1) * n] for j in range(per)],
                                axis=1) for i in range(len(terms))]
    lora = lambda t, w_ref: _dot(t.astype(BF16), w_ref[...].astype(BF16))
    ri = lax.broadcasted_iota(jnp.int32, (c, c), 0)
    ci = lax.broadcasted_iota(jnp.int32, (c, c), 1)
    strict = ri > ci
    incl = ri >= ci
    eye = (ri == ci).astype(F32)
    tri = incl.astype(BF16)
    lanes = [slice(h * HEAD_DIM, (h + 1) * HEAD_DIM) for h in range(N_HEADS)]
    group = c * CHUNK_GROUP

    def prepare_stages(ig, st):
        rows = slice(ig * group, (ig + 1) * group)
        t = {}

        def project():
            t["v"] = pa_ref[rows, 2 * D_RWKV:3 * D_RWKV]
            t["xw"] = lora(jnp.tanh(pa_ref[rows, 4 * D_RWKV:4 * D_RWKV + LORA]), wup_ref)
            t["xa"] = lora(pa_ref[rows, 4 * D_RWKV + LORA:], aup_ref)
            if first:
                vfo_ref[rows, :] = t["v"]
            else:
                t["xv"] = lora(t["v"], vdn_ref)

        def mix_value():
            if not first:
                nu = _sigmoid(v0_ref[...] + lora(t.pop("xv"), vup_ref))
                t["v"] = t["v"] + (vf_ref[rows, :] - t["v"]) * nu

        def sum_heads():
            r = pa_ref[rows, 0:D_RWKV]
            k = pa_ref[rows, D_RWKV:2 * D_RWKV]
            t["lw"] = -math.exp(-0.5) * _sigmoid(w0_ref[...] + t.pop("xw"))
            t["a"] = _sigmoid(a0_ref[...] + t.pop("xa"))
            t["kk"] = k * kk_ref[...]
            t["k2"] = k * (1.0 + (t["a"] - 1.0) * ka_ref[...])
            t["ssq"], t["rk"] = head_sums(t["kk"] * t["kk"], r * t["k2"] * rk_ref[...])

        def cumulate():
            bonus_s[rows, :] = t.pop("rk") * t["v"]
            t["kn"] = t.pop("kk") * lax.rsqrt(jnp.maximum(t.pop("ssq"), 1e-24))
            lw_hi = t["lw"].astype(BF16)
            lw_lo = (t["lw"] - lw_hi.astype(F32)).astype(BF16)
            t["cum"] = [_dot(tri, lw_hi[j * c:(j + 1) * c]) + _dot(tri, lw_lo[j * c:(j + 1) * c])
                        for j in range(CHUNK_GROUP)]

        def scale():
            r = pa_ref[rows, 0:D_RWKV]
            kb = t["kn"] * t["a"]
            vb = t["v"].astype(BF16)
            parts = {key: [] for key in ("ar", "bk", "bkend", "v", "g_end")}
            for j in range(CHUNK_GROUP):
                cr = slice(j * c, (j + 1) * c)
                cum, lw = t["cum"][j], t["lw"][cr]
                tot = cum[c - 1:c, :]
                g_inv = jnp.exp(-cum)
                g_rem = jnp.exp(tot - cum)
                ar = jnp.concatenate([-t["kn"][cr] * jnp.exp(cum - lw), r[cr] * jnp.exp(cum)],
                                     axis=0).astype(BF16)
                bk = jnp.concatenate([kb[cr] * g_inv, t["k2"][cr] * g_inv], axis=0).astype(BF16)
                bkend = jnp.concatenate([kb[cr] * g_rem, t["k2"][cr] * g_rem],
                                        axis=0).astype(BF16)
                g_end = jnp.exp(tot)
                parts["ar"] += [ar[:, ln] for ln in lanes]
                parts["bk"] += [bk[:, ln] for ln in lanes]
                parts["bkend"] += [bkend[:, ln] for ln in lanes]
                parts["v"] += [vb[cr][:, ln] for ln in lanes]
                parts["g_end"] += [g_end[:, ln] for ln in lanes]
            st.update(parts)

        return [project, mix_value, sum_heads, cumulate, scale]

    def finish_stages(ig):
        rows = slice(ig * group, (ig + 1) * group)
        t = {}

        def mean():
            t["o"] = o_s[rows, :]
            t["mean"] = head_sums(t["o"])[0] * (1.0 / HEAD_DIM)

        def variance():
            t["d"] = t.pop("o") - t.pop("mean")
            t["var"] = head_sums(t["d"] * t["d"])[0] * (1.0 / HEAD_DIM)

        def store():
            on = t.pop("d") * lax.rsqrt(t.pop("var") + GN_EPS) * gng_ref[...] + gnb_ref[...]
            gate = pa_ref[rows, 3 * D_RWKV:4 * D_RWKV]
            y_ref[rows, :] = ((on + bonus_s[rows, :]) * gate).astype(y_ref.dtype)

        return [mean, variance, store]

    n_groups = ts // group
    operands = {}
    for it in range(-2, n_groups + 1):
        streams = []
        if 0 <= it + 2 < n_groups:
            operands[it + 2] = {}
            streams.append(prepare_stages(it + 2, operands[it + 2]))
        if 0 <= it + 1 < n_groups:
            streams.append(_scan_stages_a(operands[it + 1], strict, incl, eye))
        if 0 <= it < n_groups:
            st = operands.pop(it)
            streams.append(sum((_scan_stages_b(st, j * N_HEADS, s_scr, o_s, it * group + j * c)
                                for j in range(CHUNK_GROUP)), []))
        if 0 <= it - 1 < n_groups:
            streams.append(finish_stages(it - 1))
        _interleave(streams)


def _rwkv(pa3, vf3, prm, first):
    b, s, _ = pa3.shape
    ts, c = SCAN_TILE, CHUNK
    row = lambda n: pl.BlockSpec((1, n), lambda i, j: (0, 0))
    mat = lambda m, n: pl.BlockSpec((m, n), lambda i, j: (0, 0))
    seq = lambda n: pl.BlockSpec((None, ts, n), lambda i, j: (i, j, 0))
    in_specs = [seq(N_A)]
    args = [pa3]
    if not first:
        in_specs.append(seq(D_RWKV))
        args.append(vf3)
    in_specs += [row(D_RWKV), mat(LORA, D_RWKV), row(D_RWKV), mat(LORA, D_RWKV)]
    args += [prm["w0"], prm["w_up"], prm["a0"], prm["a_up"]]
    if not first:
        in_specs += [row(D_RWKV), mat(D_RWKV, MV_LORA), mat(MV_LORA, D_RWKV)]
        args += [prm["v0"], prm["v_down"], prm["v_up"]]
    in_specs += [row(D_RWKV)] * 5 + [mat(LANES, LANES)]
    args += [prm["k_k"], prm["k_a"], prm["r_k"], prm["gn_g"], prm["gn_b"], prm["e"]]
    out_specs = [seq(D_RWKV)]
    out_shape = [jax.ShapeDtypeStruct((b, s, D_RWKV), BF16)]
    if first:
        out_specs.append(seq(D_RWKV))
        out_shape.append(jax.ShapeDtypeStruct((b, s, D_RWKV), F32))
    slab = pltpu.VMEM((ts, D_RWKV), F32)
    scratch = [pltpu.VMEM((N_HEADS, HEAD_DIM, HEAD_DIM), F32), slab, slab]
    return pl.pallas_call(
        functools.partial(_rwkv_kernel, first, ts, c),
        grid=(b, s // ts),
        in_specs=in_specs,
        out_specs=out_specs,
        out_shape=out_shape,
        scratch_shapes=scratch,
        compiler_params=pltpu.CompilerParams(
            dimension_semantics=("parallel", "arbitrary"), vmem_limit_bytes=VMEM_LIMIT),
        name="rwkv_first" if first else "rwkv",
    )(*args)


def _conv_kernel(ts, pc_ref, w_ref, b_ref, g_ref, beta_ref, y_ref, ext, shifted):
    @pl.when(pl.program_id(1) == 0)
    def _():
        ext[0:CONV_HALO, :] = jnp.zeros((CONV_HALO, D_CONV), F32)

    ext[CONV_HALO:CONV_HALO + ts, :] = pc_ref[:, 0:D_CONV]
    n_sh = shifted.shape[1]
    for r in range(1, SUBLANES):
        shifted[r - 1] = ext[r:r + n_sh, :]
    base = CONV_HALO - (CONV_WIDTH - 1)
    acc = jnp.zeros((ts, D_CONV), F32) + b_ref[...]
    for j in range(CONV_WIDTH):
        q, r = divmod(base + j, SUBLANES)
        src = ext if r == 0 else shifted.at[r - 1]
        acc = acc + w_ref[j:j + 1, :] * src[q * SUBLANES:q * SUBLANES + ts, :]
    ext[0:CONV_HALO, :] = ext[ts:ts + CONV_HALO, :]
    mean = jnp.mean(acc, axis=-1, keepdims=True)
    d = acc - mean
    var = jnp.mean(d * d, axis=-1, keepdims=True)
    cn = d * lax.rsqrt(var + LN_EPS) * g_ref[...] + beta_ref[...]
    y_ref[...] = ((cn * _sigmoid(cn)) * pc_ref[:, D_CONV:]).astype(y_ref.dtype)


def _conv(pc3, prm):
    b, s, _ = pc3.shape
    ts = CONV_TILE
    row = lambda n: pl.BlockSpec((1, n), lambda i, j: (0, 0))
    return pl.pallas_call(
        functools.partial(_conv_kernel, ts),
        grid=(b, s // ts),
        in_specs=[
            pl.BlockSpec((None, ts, N_C), lambda i, j: (i, j, 0)),
            pl.BlockSpec((CONV_HALO, D_CONV), lambda i, j: (0, 0)),
            row(D_CONV), row(D_CONV), row(D_CONV),
        ],
        out_specs=pl.BlockSpec((None, ts, D_CONV), lambda i, j: (i, j, 0)),
        out_shape=jax.ShapeDtypeStruct((b, s, D_CONV), BF16),
        scratch_shapes=[
            pltpu.VMEM((ts + CONV_HALO, D_CONV), F32),
            pltpu.VMEM((SUBLANES - 1, ts + CONV_HALO - SUBLANES, D_CONV), F32),
        ],
        compiler_params=pltpu.CompilerParams(
            dimension_semantics=("parallel", "arbitrary"), vmem_limit_bytes=VMEM_LIMIT),
        name="conv",
    )(pc3, prm["dw_w"], prm["dw_b"], prm["ln_g"], prm["ln_b"])


def _outproj_kernel(x_ref, yr_ref, yc_ref, wr_ref, wc_ref, g_ref, o_ref):
    out = _dot(yr_ref[...], wr_ref[...]) + _dot(yc_ref[...], wc_ref[...])
    ms = jnp.mean(out * out, axis=-1, keepdims=True)
    o_ref[...] = x_ref[...] + out * lax.rsqrt(ms + RMS_EPS) * g_ref[...]


def _outproj(x2, yr, yc, wr, wc, g):
    rows = x2.shape[0]
    tm = OUT_TILE
    return pl.pallas_call(
        _outproj_kernel,
        grid=(rows // tm,),
        in_specs=[
            pl.BlockSpec((tm, D_MODEL), lambda i: (i, 0)),
            pl.BlockSpec((tm, D_RWKV), lambda i: (i, 0)),
            pl.BlockSpec((tm, D_CONV), lambda i: (i, 0)),
            pl.BlockSpec((D_RWKV, D_MODEL), lambda i: (0, 0)),
            pl.BlockSpec((D_CONV, D_MODEL), lambda i: (0, 0)),
            pl.BlockSpec((1, D_MODEL), lambda i: (0, 0)),
        ],
        out_specs=pl.BlockSpec((tm, D_MODEL), lambda i: (i, 0)),
        out_shape=jax.ShapeDtypeStruct((rows, D_MODEL), F32),
        compiler_params=pltpu.CompilerParams(
            dimension_semantics=("parallel",), vmem_limit_bytes=VMEM_LIMIT),
        name="outproj",
    )(x2, yr, yc, wr, wc, g)


def kernel(x, pre_g, post_g, w_in, mu_shift, w0, w_up, a0, a_up, v0, v_down, v_up, k_k, k_a, r_k,
           gn_g, gn_b, dw_w, dw_b, ln_g, ln_b, w_out):
    b, s, _ = x.shape
    depth = w_in.shape[0]
    assert s % SCAN_TILE == 0 and s % CONV_TILE == 0 and s % ROW_TILE == 0
    assert (b * s) % OUT_TILE == 0
    assert SCAN_TILE % (CHUNK * CHUNK_GROUP) == 0 and CHUNK >= 8
    assert CONV_TILE >= CONV_HALO >= CONV_WIDTH - 1
    head = jnp.arange(LANES, dtype=jnp.int32) // HEAD_DIM
    e = (head[:, None] == head[None, :]).astype(BF16)
    w_in_bf = w_in.astype(BF16)
    w_out_bf = w_out.astype(BF16)
    dw_w_pad = jnp.pad(dw_w, ((0, 0), (0, CONV_HALO - CONV_WIDTH), (0, 0)))
    r1 = lambda t: t.reshape(1, -1)

    x2 = x.reshape(b * s, D_MODEL)
    v_first = None
    for l in range(depth):
        pa, pc = _inproj(x2, r1(pre_g[l]), w_in_bf[l, :, :N_A], w_in_bf[l, :, N_A:],
                         r1(mu_shift[l]), s)
        prm = dict(w0=r1(w0[l]), w_up=w_up[l], a0=r1(a0[l]), a_up=a_up[l],
                   k_k=r1(k_k[l]), k_a=r1(k_a[l]), r_k=r1(r_k[l]), gn_g=r1(gn_g[l]),
                   gn_b=r1(gn_b[l]), e=e, dw_w=dw_w_pad[l], dw_b=r1(dw_b[l]), ln_g=r1(ln_g[l]),
                   ln_b=r1(ln_b[l]))
        pa3 = pa.reshape(b, s, N_A)
        if l == 0:
            yr, v_first = _rwkv(pa3, None, prm, True)
        else:
            prm.update(v0=r1(v0[l - 1]), v_down=v_down[l - 1], v_up=v_up[l - 1])
            yr = _rwkv(pa3, v_first, prm, False)[0]
        yc = _conv(pc.reshape(b, s, N_C), prm)
        x2 = _outproj(x2, yr.reshape(b * s, D_RWKV), yc.reshape(b * s, D_CONV),
                      w_out_bf[l, :D_RWKV], w_out_bf[l, D_RWKV:], r1(post_g[l]))
    return x2.reshape(b, s, D_MODEL)
```

```python
import functools
import math

import jax
import jax.numpy as jnp
from jax import lax
from jax.experimental import pallas as pl
from jax.experimental.pallas import tpu as pltpu

D_MODEL = 1024
D_RWKV = 512
HEAD_DIM = 64
N_HEADS = D_RWKV // HEAD_DIM
D_CONV = 512
CONV_WIDTH = 31
LORA = 64
MV_LORA = 32
N_A = 4 * D_RWKV + 2 * LORA
N_B = 3 * D_CONV
N_C = 2 * D_CONV
RMS_EPS = 1e-6
GN_EPS = 64e-5
LN_EPS = 1e-5

SUBLANES = 8
LANES = 128
CHUNK = 64
CHUNK_GROUP = 2
ROW_TILE = 512
OUT_TILE = 1024
SCAN_TILE = 1024
SCAN_SEQS = 1
CONV_TILE = 256
CONV_HALO = 32
VMEM_LIMIT = 48 * 1024 * 1024

F32 = jnp.float32
BF16 = jnp.bfloat16


def _dot(a, b, dims=((1,), (0,))):
    return lax.dot_general(a, b, (dims, ((), ())), preferred_element_type=F32)


def _sigmoid(x):
    return 1.0 / (1.0 + jnp.exp(-x))


def _inproj_kernel(tiles_per_seq, x_ref, g_ref, wa_ref, wb_ref, mu_ref, pa_ref, pc_ref, carry):
    tm = x_ref.shape[0]

    @pl.when(pl.program_id(0) % tiles_per_seq == 0)
    def _():
        carry[...] = jnp.zeros_like(carry)

    x = x_ref[...]
    ms = jnp.mean(x * x, axis=-1, keepdims=True)
    h = (x * lax.rsqrt(ms + RMS_EPS) * g_ref[...]).astype(BF16)
    pa = _dot(h, wa_ref[...])
    row = lax.broadcasted_iota(jnp.int32, (tm, 1), 0)
    prev = jnp.where(row == 0, carry[0:1, :], pltpu.roll(pa, 1, 0))
    carry[0:1, :] = pa[tm - 1:tm, :]
    ps = pa + (prev - pa) * mu_ref[...]
    z = ps[:, 3 * D_RWKV:4 * D_RWKV]
    pa_ref[:, 0:3 * D_RWKV] = ps[:, 0:3 * D_RWKV]
    pa_ref[:, 3 * D_RWKV:4 * D_RWKV] = z * _sigmoid(z)
    pa_ref[:, 4 * D_RWKV:] = ps[:, 4 * D_RWKV:]
    pb = _dot(h, wb_ref[...])
    zc = pb[:, 2 * D_CONV:]
    pc_ref[:, 0:D_CONV] = pb[:, 0:D_CONV] * _sigmoid(pb[:, D_CONV:2 * D_CONV])
    pc_ref[:, D_CONV:] = zc * _sigmoid(zc)


def _inproj(x2, g, wa, wb, mu, seq_len):
    rows = x2.shape[0]
    tm = ROW_TILE
    return pl.pallas_call(
        functools.partial(_inproj_kernel, seq_len // tm),
        grid=(rows // tm,),
        in_specs=[
            pl.BlockSpec((tm, D_MODEL), lambda i: (i, 0)),
            pl.BlockSpec((1, D_MODEL), lambda i: (0, 0)),
            pl.BlockSpec((D_MODEL, N_A), lambda i: (0, 0)),
            pl.BlockSpec((D_MODEL, N_B), lambda i: (0, 0)),
            pl.BlockSpec((1, N_A), lambda i: (0, 0)),
        ],
        out_specs=[
            pl.BlockSpec((tm, N_A), lambda i: (i, 0)),
            pl.BlockSpec((tm, N_C), lambda i: (i, 0)),
        ],
        out_shape=[
            jax.ShapeDtypeStruct((rows, N_A), F32),
            jax.ShapeDtypeStruct((rows, N_C), F32),
        ],
        scratch_shapes=[pltpu.VMEM((SUBLANES, N_A), F32)],
        compiler_params=pltpu.CompilerParams(
            dimension_semantics=("arbitrary",), vmem_limit_bytes=VMEM_LIMIT),
        name="inproj",
    )(x2, g, wa, wb, mu)


def _interleave(streams):
    tagged = []
    for s, stages in enumerate(streams):
        tagged += [((k + 0.5) / len(stages), s, stage) for k, stage in enumerate(stages)]
    for _, _, stage in sorted(tagged, key=lambda entry: entry[:2]):
        stage()


def _scan_stages_a(st, strict, incl, eye):
    heads = range(len(st["v"]))
    c = st["v"][0].shape[0]

    def gram():
        st["m"] = [_dot(st["ar"][h], st["bk"][h]) for h in heads]

    def square():
        m = st.pop("m")
        a_ab = [jnp.where(strict, m[h][:c, :c], 0.0) for h in heads]
        a_k = [jnp.concatenate([jnp.where(strict, m[h][:c, c:], 0.0),
                                jnp.where(incl, m[h][c:, c:], 0.0)], axis=0).astype(BF16)
               for h in heads]
        st["a_rb"] = [jnp.where(incl, m[h][c:, :c], 0.0).astype(BF16) for h in heads]
        p = [a_ab[h].astype(BF16) for h in heads]
        st["t"] = [eye + a_ab[h] for h in heads]
        st["p"] = [_dot(p[h], p[h]).astype(BF16) for h in heads]
        st["akv"] = [_dot(a_k[h], st["v"][h]) for h in heads]

    def double():
        p, t = st["p"], st["t"]
        pt = [_dot(jnp.concatenate([p[h], t[h].astype(BF16)], axis=0), p[h]) for h in heads]
        st["t"] = [t[h] + pt[h][c:] for h in heads]
        st["p"] = [pt[h][:c].astype(BF16) for h in heads]

    def last():
        p, t = st.pop("p"), st["t"]
        st["t"] = [(t[h] + _dot(t[h].astype(BF16), p[h])).astype(BF16) for h in heads]

    def apply_t():
        t, akv = st.pop("t"), st.pop("akv")
        st["at2"] = [_dot(t[h], st["ar"][h][:c]).astype(BF16) for h in heads]
        st["ua"] = [_dot(t[h], akv[h][:c].astype(BF16)) for h in heads]
        st["yv"] = [akv[h][c:] for h in heads]

    return [gram, square] + [double] * (c.bit_length() - 3) + [last, apply_t]


def _scan_stages_b(st, lo, s_scr, o_s, row0):
    heads = range(s_scr.shape[0])
    c = st["v"][0].shape[0]
    s0, z = {}, {}

    def read_state():
        for h in heads:
            s0[h] = s_scr[h]
            lhs = jnp.concatenate([st["at2"][lo + h], st["ar"][lo + h][c:]], axis=0)
            z[h] = _dot(lhs, s0[h].astype(BF16), ((1,), (1,)))

    def write_state():
        u = [(z[h][:c] + st["ua"][lo + h]).astype(BF16) for h in heads]
        y = [z[h][c:] + st["yv"][lo + h] + _dot(st["a_rb"][lo + h], u[h]) for h in heads]
        for h in heads:
            uv = jnp.concatenate([u[h], st["v"][lo + h]], axis=0)
            s_scr[h] = s0[h] * st["g_end"][lo + h] + _dot(uv, st["bkend"][lo + h], ((0,), (0,)))
            o_s[row0:row0 + c, h * HEAD_DIM:(h + 1) * HEAD_DIM] = y[h]

    return [read_state, write_state]


def _rwkv_kernel(first, nb, ts, c, *refs):
    refs = list(refs)
    pa_ref = refs.pop(0)
    vf_ref = None if first else refs.pop(0)
    w0_ref, wup_ref, a0_ref, aup_ref = (refs.pop(0) for _ in range(4))
    if not first:
        v0_ref, vdn_ref, vup_ref = (refs.pop(0) for _ in range(3))
    kk_ref, ka_ref, rk_ref, gng_ref, gnb_ref, e_ref = (refs.pop(0) for _ in range(6))
    y_ref = refs.pop(0)
    vfo_ref = refs.pop(0) if first else None
    s_scr, o_s, bonus_s = refs

    @pl.when(pl.program_id(1) == 0)
    def _():
        s_scr[...] = jnp.zeros_like(s_scr)

    e = e_ref[...]

    def head_sums(*terms):
        n = terms[0].shape[0]
        tiles = [t[:, lo:lo + LANES].astype(BF16) for t in terms for lo in range(0, D_RWKV, LANES)]
        sums = _dot(jnp.concatenate(tiles, axis=0), e)
        per = D_RWKV // LANES
        return [jnp.concatenate([sums[(i * per + j) * n:(i * per + j + 1) * n] for j in range(per)],
                                axis=1) for i in range(len(terms))]
    lora = lambda t, w_ref: _dot(t.astype(BF16), w_ref[...].astype(BF16))
    ri = lax.broadcasted_iota(jnp.int32, (c, c), 0)
    ci = lax.broadcasted_iota(jnp.int32, (c, c), 1)
    strict = ri > ci
    incl = ri >= ci
    eye = (ri == ci).astype(F32)
    tri = incl.astype(BF16)
    lanes = [slice(h * HEAD_DIM, (h + 1) * HEAD_DIM) for h in range(N_HEADS)]
    group = c * CHUNK_GROUP

    def prepare_stages(bi, ig, st):
        rows = slice(ig * group, (ig + 1) * group)
        pa, bonus = pa_ref.at[bi], bonus_s.at[bi]
        t = {}

        def project():
            t["v"] = pa[rows,2 * D_RWKV:3 * D_RWKV]
            t["xw"] = lora(jnp.tanh(pa[rows,4 * D_RWKV:4 * D_RWKV + LORA]), wup_ref)
            t["xa"] = lora(pa[rows,4 * D_RWKV + LORA:], aup_ref)
            if first:
                vfo_ref[bi, rows, :] = t["v"]
            else:
                t["xv"] = lora(t["v"], vdn_ref)

        def mix_value():
            if not first:
                nu = _sigmoid(v0_ref[...] + lora(t.pop("xv"), vup_ref))
                t["v"] = t["v"] + (vf_ref[bi, rows, :] - t["v"]) * nu

        def sum_heads():
            r = pa[rows,0:D_RWKV]
            k = pa[rows,D_RWKV:2 * D_RWKV]
            t["lw"] = -math.exp(-0.5) * _sigmoid(w0_ref[...] + t.pop("xw"))
            t["a"] = _sigmoid(a0_ref[...] + t.pop("xa"))
            t["kk"] = k * kk_ref[...]
            t["k2"] = k * (1.0 + (t["a"] - 1.0) * ka_ref[...])
            t["ssq"], t["rk"] = head_sums(t["kk"] * t["kk"], r * t["k2"] * rk_ref[...])

        def cumulate():
            bonus[rows, :] = t.pop("rk") * t["v"]
            t["kn"] = t.pop("kk") * lax.rsqrt(jnp.maximum(t.pop("ssq"), 1e-24))
            lw_hi = t["lw"].astype(BF16)
            lw_lo = (t["lw"] - lw_hi.astype(F32)).astype(BF16)
            t["cum"] = [_dot(tri, lw_hi[j * c:(j + 1) * c]) + _dot(tri, lw_lo[j * c:(j + 1) * c])
                        for j in range(CHUNK_GROUP)]

        def scale():
            r = pa[rows,0:D_RWKV]
            kb = t["kn"] * t["a"]
            vb = t["v"].astype(BF16)
            parts = {key: [] for key in ("ar", "bk", "bkend", "v", "g_end")}
            for j in range(CHUNK_GROUP):
                cr = slice(j * c, (j + 1) * c)
                cum, lw = t["cum"][j], t["lw"][cr]
                tot = cum[c - 1:c, :]
                g_inv = jnp.exp(-cum)
                g_rem = jnp.exp(tot - cum)
                ar = jnp.concatenate([-t["kn"][cr] * jnp.exp(cum - lw), r[cr] * jnp.exp(cum)],
                                     axis=0).astype(BF16)
                bk = jnp.concatenate([kb[cr] * g_inv, t["k2"][cr] * g_inv], axis=0).T.astype(BF16)
                bkend = jnp.concatenate([kb[cr] * g_rem, t["k2"][cr] * g_rem],
                                        axis=0).astype(BF16)
                g_end = jnp.exp(tot)
                parts["ar"] += [ar[:, ln] for ln in lanes]
                parts["bk"] += [bk[ln, :] for ln in lanes]
                parts["bkend"] += [bkend[:, ln] for ln in lanes]
                parts["v"] += [vb[cr][:, ln] for ln in lanes]
                parts["g_end"] += [g_end[:, ln] for ln in lanes]
            st.update(parts)

        return [project, mix_value, sum_heads, cumulate, scale]

    def finish_stages(bi, ig):
        rows = slice(ig * group, (ig + 1) * group)
        pa, bonus = pa_ref.at[bi], bonus_s.at[bi]
        t = {}

        def mean():
            t["o"] = o_s[bi, rows, :]
            t["mean"] = head_sums(t["o"])[0] * (1.0 / HEAD_DIM)

        def variance():
            t["d"] = t.pop("o") - t.pop("mean")
            t["var"] = head_sums(t["d"] * t["d"])[0] * (1.0 / HEAD_DIM)

        def store():
            on = t.pop("d") * lax.rsqrt(t.pop("var") + GN_EPS) * gng_ref[...] + gnb_ref[...]
            gate = pa[rows,3 * D_RWKV:4 * D_RWKV]
            y_ref[bi, rows, :] = ((on + bonus[rows, :]) * gate).astype(y_ref.dtype)

        return [mean, variance, store]

    n_groups = ts // group
    operands = {}
    for it in range(-2, n_groups + 1):
        streams = []
        for bi in range(nb):
            if 0 <= it + 2 < n_groups:
                operands[bi, it + 2] = {}
                streams.append(prepare_stages(bi, it + 2, operands[bi, it + 2]))
            if 0 <= it + 1 < n_groups:
                streams.append(_scan_stages_a(operands[bi, it + 1], strict, incl, eye))
            if 0 <= it < n_groups:
                st = operands.pop((bi, it))
                streams.append(sum((_scan_stages_b(st, j * N_HEADS, s_scr.at[bi], o_s.at[bi],
                                                   it * group + j * c)
                                    for j in range(CHUNK_GROUP)), []))
            if 0 <= it - 1 < n_groups:
                streams.append(finish_stages(bi, it - 1))
        _interleave(streams)


def _rwkv(pa3, vf3, prm, first):
    b, s, _ = pa3.shape
    nb, ts, c = SCAN_SEQS, SCAN_TILE, CHUNK
    row = lambda n: pl.BlockSpec((1, n), lambda i, j: (0, 0))
    mat = lambda m, n: pl.BlockSpec((m, n), lambda i, j: (0, 0))
    seq = lambda n: pl.BlockSpec((nb, ts, n), lambda i, j: (i, j, 0))
    in_specs = [seq(N_A)]
    args = [pa3]
    if not first:
        in_specs.append(seq(D_RWKV))
        args.append(vf3)
    in_specs += [row(D_RWKV), mat(LORA, D_RWKV), row(D_RWKV), mat(LORA, D_RWKV)]
    args += [prm["w0"], prm["w_up"], prm["a0"], prm["a_up"]]
    if not first:
        in_specs += [row(D_RWKV), mat(D_RWKV, MV_LORA), mat(MV_LORA, D_RWKV)]
        args += [prm["v0"], prm["v_down"], prm["v_up"]]
    in_specs += [row(D_RWKV)] * 5 + [mat(LANES, LANES)]
    args += [prm["k_k"], prm["k_a"], prm["r_k"], prm["gn_g"], prm["gn_b"], prm["e"]]
    out_specs = [seq(D_RWKV)]
    out_shape = [jax.ShapeDtypeStruct((b, s, D_RWKV), BF16)]
    if first:
        out_specs.append(seq(D_RWKV))
        out_shape.append(jax.ShapeDtypeStruct((b, s, D_RWKV), F32))
    slab = pltpu.VMEM((nb, ts, D_RWKV), F32)
    scratch = [pltpu.VMEM((nb, N_HEADS, HEAD_DIM, HEAD_DIM), F32), slab, slab]
    return pl.pallas_call(
        functools.partial(_rwkv_kernel, first, nb, ts, c),
        grid=(b // nb, s // ts),
        in_specs=in_specs,
        out_specs=out_specs,
        out_shape=out_shape,
        scratch_shapes=scratch,
        compiler_params=pltpu.CompilerParams(
            dimension_semantics=("parallel", "arbitrary"), vmem_limit_bytes=VMEM_LIMIT),
        name="rwkv_first" if first else "rwkv",
    )(*args)


def _conv_kernel(ts, pc_ref, w_ref, b_ref, g_ref, beta_ref, y_ref, ext, shifted):
    @pl.when(pl.program_id(1) == 0)
    def _():
        ext[0:CONV_HALO, :] = jnp.zeros((CONV_HALO, D_CONV), F32)

    ext[CONV_HALO:CONV_HALO + ts, :] = pc_ref[:, 0:D_CONV]
    n_sh = shifted.shape[1]
    for r in range(1, SUBLANES):
        shifted[r - 1] = ext[r:r + n_sh, :]
    base = CONV_HALO - (CONV_WIDTH - 1)
    acc = jnp.zeros((ts, D_CONV), F32) + b_ref[...]
    for j in range(CONV_WIDTH):
        q, r = divmod(base + j, SUBLANES)
        src = ext if r == 0 else shifted.at[r - 1]
        acc = acc + w_ref[j:j + 1, :] * src[q * SUBLANES:q * SUBLANES + ts, :]
    ext[0:CONV_HALO, :] = ext[ts:ts + CONV_HALO, :]
    mean = jnp.mean(acc, axis=-1, keepdims=True)
    d = acc - mean
    var = jnp.mean(d * d, axis=-1, keepdims=True)
    cn = d * lax.rsqrt(var + LN_EPS) * g_ref[...] + beta_ref[...]
    y_ref[...] = ((cn * _sigmoid(cn)) * pc_ref[:, D_CONV:]).astype(y_ref.dtype)


def _conv(pc3, prm):
    b, s, _ = pc3.shape
    ts = CONV_TILE
    row = lambda n: pl.BlockSpec((1, n), lambda i, j: (0, 0))
    return pl.pallas_call(
        functools.partial(_conv_kernel, ts),
        grid=(b, s // ts),
        in_specs=[
            pl.BlockSpec((None, ts, N_C), lambda i, j: (i, j, 0)),
            pl.BlockSpec((CONV_HALO, D_CONV), lambda i, j: (0, 0)),
            row(D_CONV), row(D_CONV), row(D_CONV),
        ],
        out_specs=pl.BlockSpec((None, ts, D_CONV), lambda i, j: (i, j, 0)),
        out_shape=jax.ShapeDtypeStruct((b, s, D_CONV), BF16),
        scratch_shapes=[
            pltpu.VMEM((ts + CONV_HALO, D_CONV), F32),
            pltpu.VMEM((SUBLANES - 1, ts + CONV_HALO - SUBLANES, D_CONV), F32),
        ],
        compiler_params=pltpu.CompilerParams(
            dimension_semantics=("parallel", "arbitrary"), vmem_limit_bytes=VMEM_LIMIT),
        name="conv",
    )(pc3, prm["dw_w"], prm["dw_b"], prm["ln_g"], prm["ln_b"])


def _outproj_kernel(x_ref, yr_ref, yc_ref, wr_ref, wc_ref, g_ref, o_ref):
    out = _dot(yr_ref[...], wr_ref[...]) + _dot(yc_ref[...], wc_ref[...])
    ms = jnp.mean(out * out, axis=-1, keepdims=True)
    o_ref[...] = x_ref[...] + out * lax.rsqrt(ms + RMS_EPS) * g_ref[...]


def _outproj(x2, yr, yc, wr, wc, g):
    rows = x2.shape[0]
    tm = OUT_TILE
    return pl.pallas_call(
        _outproj_kernel,
        grid=(rows // tm,),
        in_specs=[
            pl.BlockSpec((tm, D_MODEL), lambda i: (i, 0)),
            pl.BlockSpec((tm, D_RWKV), lambda i: (i, 0)),
            pl.BlockSpec((tm, D_CONV), lambda i: (i, 0)),
            pl.BlockSpec((D_RWKV, D_MODEL), lambda i: (0, 0)),
            pl.BlockSpec((D_CONV, D_MODEL), lambda i: (0, 0)),
            pl.BlockSpec((1, D_MODEL), lambda i: (0, 0)),
        ],
        out_specs=pl.BlockSpec((tm, D_MODEL), lambda i: (i, 0)),
        out_shape=jax.ShapeDtypeStruct((rows, D_MODEL), F32),
        compiler_params=pltpu.CompilerParams(
            dimension_semantics=("parallel",), vmem_limit_bytes=VMEM_LIMIT),
        name="outproj",
    )(x2, yr, yc, wr, wc, g)


def kernel(x, pre_g, post_g, w_in, mu_shift, w0, w_up, a0, a_up, v0, v_down, v_up, k_k, k_a, r_k,
           gn_g, gn_b, dw_w, dw_b, ln_g, ln_b, w_out):
    b, s, _ = x.shape
    depth = w_in.shape[0]
    assert s % SCAN_TILE == 0 and s % CONV_TILE == 0 and s % ROW_TILE == 0 and b % SCAN_SEQS == 0
    assert (b * s) % OUT_TILE == 0
    assert SCAN_TILE % (CHUNK * CHUNK_GROUP) == 0 and CHUNK >= 8
    assert CONV_TILE >= CONV_HALO >= CONV_WIDTH - 1
    head = jnp.arange(LANES, dtype=jnp.int32) // HEAD_DIM
    e = (head[:, None] == head[None, :]).astype(BF16)
    w_in_bf = w_in.astype(BF16)
    w_out_bf = w_out.astype(BF16)
    dw_w_pad = jnp.pad(dw_w, ((0, 0), (0, CONV_HALO - CONV_WIDTH), (0, 0)))
    r1 = lambda t: t.reshape(1, -1)

    x2 = x.reshape(b * s, D_MODEL)
    v_first = None
    for l in range(depth):
        pa, pc = _inproj(x2, r1(pre_g[l]), w_in_bf[l, :, :N_A], w_in_bf[l, :, N_A:],
                         r1(mu_shift[l]), s)
        prm = dict(w0=r1(w0[l]), w_up=w_up[l], a0=r1(a0[l]), a_up=a_up[l],
                   k_k=r1(k_k[l]), k_a=r1(k_a[l]), r_k=r1(r_k[l]), gn_g=r1(gn_g[l]),
                   gn_b=r1(gn_b[l]), e=e, dw_w=dw_w_pad[l], dw_b=r1(dw_b[l]), ln_g=r1(ln_g[l]),
                   ln_b=r1(ln_b[l]))
        pa3 = pa.reshape(b, s, N_A)
        if l == 0:
            yr, v_first = _rwkv(pa3, None, prm, True)
        else:
            prm.update(v0=r1(v0[l - 1]), v_down=v_down[l - 1], v_up=v_up[l - 1])
            yr = _rwkv(pa3, v_first, prm, False)[0]
        yc = _conv(pc.reshape(b, s, N_C), prm)
        x2 = _outproj(x2, yr.reshape(b * s, D_RWKV), yc.reshape(b * s, D_CONV),
                      w_out_bf[l, :D_RWKV], w_out_bf[l, D_RWKV:], r1(post_g[l]))
    return x2.reshape(b, s, D_MODEL)
```

```python
import functools
import math

import jax
import jax.numpy as jnp
from jax import lax
from jax.experimental import pallas as pl
from jax.experimental.pallas import tpu as pltpu

D_MODEL = 1024
D_RWKV = 512
HEAD_DIM = 64
N_HEADS = D_RWKV // HEAD_DIM
D_CONV = 512
CONV_WIDTH = 31
LORA = 64
MV_LORA = 32
N_A = 4 * D_RWKV + 2 * LORA
N_B = 3 * D_CONV
N_C = 2 * D_CONV
RMS_EPS = 1e-6
GN_EPS = 64e-5
LN_EPS = 1e-5

SUBLANES = 8
LANES = 128
CHUNK = 64
CHUNK_GROUP = 2
ROW_TILE = 512
OUT_TILE = 1024
SCAN_TILE = 1024
SCAN_SEQS = 1
CONV_TILE = 256
CONV_HALO = 32
VMEM_LIMIT = 48 * 1024 * 1024

F32 = jnp.float32
BF16 = jnp.bfloat16


def _dot(a, b, dims=((1,), (0,))):
    return lax.dot_general(a, b, (dims, ((), ())), preferred_element_type=F32)


def _sigmoid(x):
    return 1.0 / (1.0 + jnp.exp(-x))


def _inproj_kernel(tiles_per_seq, x_ref, g_ref, wa_ref, wb_ref, mu_ref, pa_ref, pc_ref, carry):
    tm = x_ref.shape[0]

    @pl.when(pl.program_id(0) % tiles_per_seq == 0)
    def _():
        carry[...] = jnp.zeros_like(carry)

    x = x_ref[...]
    ms = jnp.mean(x * x, axis=-1, keepdims=True)
    h = (x * lax.rsqrt(ms + RMS_EPS) * g_ref[...]).astype(BF16)
    pa = _dot(h, wa_ref[...])
    row = lax.broadcasted_iota(jnp.int32, (tm, 1), 0)
    prev = jnp.where(row == 0, carry[0:1, :], pltpu.roll(pa, 1, 0))
    carry[0:1, :] = pa[tm - 1:tm, :]
    ps = pa + (prev - pa) * mu_ref[...]
    z = ps[:, 3 * D_RWKV:4 * D_RWKV]
    pa_ref[:, 0:3 * D_RWKV] = ps[:, 0:3 * D_RWKV]
    pa_ref[:, 3 * D_RWKV:4 * D_RWKV] = z * _sigmoid(z)
    pa_ref[:, 4 * D_RWKV:] = ps[:, 4 * D_RWKV:]
    pb = _dot(h, wb_ref[...])
    zc = pb[:, 2 * D_CONV:]
    pc_ref[:, 0:D_CONV] = pb[:, 0:D_CONV] * _sigmoid(pb[:, D_CONV:2 * D_CONV])
    pc_ref[:, D_CONV:] = zc * _sigmoid(zc)


def _inproj(x2, g, wa, wb, mu, seq_len):
    rows = x2.shape[0]
    tm = ROW_TILE
    return pl.pallas_call(
        functools.partial(_inproj_kernel, seq_len // tm),
        grid=(rows // tm,),
        in_specs=[
            pl.BlockSpec((tm, D_MODEL), lambda i: (i, 0)),
            pl.BlockSpec((1, D_MODEL), lambda i: (0, 0)),
            pl.BlockSpec((D_MODEL, N_A), lambda i: (0, 0)),
            pl.BlockSpec((D_MODEL, N_B), lambda i: (0, 0)),
            pl.BlockSpec((1, N_A), lambda i: (0, 0)),
        ],
        out_specs=[
            pl.BlockSpec((tm, N_A), lambda i: (i, 0)),
            pl.BlockSpec((tm, N_C), lambda i: (i, 0)),
        ],
        out_shape=[
            jax.ShapeDtypeStruct((rows, N_A), F32),
            jax.ShapeDtypeStruct((rows, N_C), F32),
        ],
        scratch_shapes=[pltpu.VMEM((SUBLANES, N_A), F32)],
        compiler_params=pltpu.CompilerParams(
            dimension_semantics=("arbitrary",), vmem_limit_bytes=VMEM_LIMIT),
        name="inproj",
    )(x2, g, wa, wb, mu)


def _interleave(streams):
    tagged = []
    for s, stages in enumerate(streams):
        tagged += [((k + 0.5) / len(stages), s, stage) for k, stage in enumerate(stages)]
    for _, _, stage in sorted(tagged, key=lambda entry: entry[:2]):
        stage()


def _scan_stages_a(st, strict, incl, eye):
    heads = range(len(st["v"]))
    c = st["v"][0].shape[0]

    def gram():
        st["m"] = [_dot(st["ar"][h], st["bk"][h]) for h in heads]

    def square():
        m = st.pop("m")
        a_ab = [jnp.where(strict, m[h][:c, :c], 0.0) for h in heads]
        a_k = [jnp.concatenate([jnp.where(strict, m[h][:c, c:], 0.0),
                                jnp.where(incl, m[h][c:, c:], 0.0)], axis=0).astype(BF16)
               for h in heads]
        st["a_rb"] = [jnp.where(incl, m[h][c:, :c], 0.0).astype(BF16) for h in heads]
        p = [a_ab[h].astype(BF16) for h in heads]
        st["t"] = [eye + a_ab[h] for h in heads]
        st["p"] = [_dot(p[h], p[h]).astype(BF16) for h in heads]
        st["akv"] = [_dot(a_k[h], st["v"][h]) for h in heads]

    def double():
        p, t = st["p"], st["t"]
        pt = [_dot(jnp.concatenate([p[h], t[h].astype(BF16)], axis=0), p[h]) for h in heads]
        st["t"] = [t[h] + pt[h][c:] for h in heads]
        st["p"] = [pt[h][:c].astype(BF16) for h in heads]

    def last():
        p, t = st.pop("p"), st["t"]
        st["t"] = [(t[h] + _dot(t[h].astype(BF16), p[h])).astype(BF16) for h in heads]

    def apply_t():
        t, akv = st.pop("t"), st.pop("akv")
        st["at2"] = [_dot(t[h], st["ar"][h][:c]).astype(BF16) for h in heads]
        st["ua"] = [_dot(t[h], akv[h][:c].astype(BF16)) for h in heads]
        st["yv"] = [akv[h][c:] for h in heads]

    return [gram, square] + [double] * (c.bit_length() - 3) + [last, apply_t]


def _scan_stages_b(st, lo, s_scr, o_s, row0):
    heads = range(s_scr.shape[0])
    c = st["v"][0].shape[0]
    s0, z = {}, {}

    def read_state():
        for h in heads:
            s0[h] = s_scr[h]
            lhs = jnp.concatenate([st["at2"][lo + h], st["ar"][lo + h][c:]], axis=0)
            z[h] = _dot(lhs, s0[h].astype(BF16), ((1,), (1,)))

    def write_state():
        u = [(z[h][:c] + st["ua"][lo + h]).astype(BF16) for h in heads]
        y = [z[h][c:] + st["yv"][lo + h] + _dot(st["a_rb"][lo + h], u[h]) for h in heads]
        for h in heads:
            uv = jnp.concatenate([u[h], st["v"][lo + h]], axis=0)
            s_scr[h] = s0[h] * st["g_end"][lo + h] + _dot(uv, st["bkend"][lo + h], ((0,), (0,)))
            o_s[row0:row0 + c, h * HEAD_DIM:(h + 1) * HEAD_DIM] = y[h]

    return [read_state, write_state]


def _rwkv_kernel(first, nb, ts, c, *refs):
    refs = list(refs)
    pa_ref = refs.pop(0)
    vf_ref = None if first else refs.pop(0)
    w0_ref, wup_ref, a0_ref, aup_ref = (refs.pop(0) for _ in range(4))
    if not first:
        v0_ref, vdn_ref, vup_ref = (refs.pop(0) for _ in range(3))
    kk_ref, ka_ref, rk_ref, gng_ref, gnb_ref, e_ref = (refs.pop(0) for _ in range(6))
    y_ref = refs.pop(0)
    vfo_ref = refs.pop(0) if first else None
    s_scr, o_s, bonus_s = refs

    @pl.when(pl.program_id(1) == 0)
    def _():
        s_scr[...] = jnp.zeros_like(s_scr)

    e = e_ref[...]

    def head_sums(*terms):
        n = terms[0].shape[0]
        tiles = [t[:, lo:lo + LANES].astype(BF16) for t in terms for lo in range(0, D_RWKV, LANES)]
        sums = _dot(jnp.concatenate(tiles, axis=0), e)
        per = D_RWKV // LANES
        return [jnp.concatenate([sums[(i * per + j) * n:(i * per + j + 1) * n] for j in range(per)],
                                axis=1) for i in range(len(terms))]
    lora = lambda t, w_ref: _dot(t.astype(BF16), w_ref[...].astype(BF16))
    ri = lax.broadcasted_iota(jnp.int32, (c, c), 0)
    ci = lax.broadcasted_iota(jnp.int32, (c, c), 1)
    strict = ri > ci
    incl = ri >= ci
    eye = (ri == ci).astype(F32)
    tri = incl.astype(BF16)
    lanes = [slice(h * HEAD_DIM, (h + 1) * HEAD_DIM) for h in range(N_HEADS)]
    group = c * CHUNK_GROUP

    def prepare_stages(bi, ig, st):
        rows = slice(ig * group, (ig + 1) * group)
        pa, bonus = pa_ref.at[bi], bonus_s.at[bi]
        t = {}

        def project():
            t["v"] = pa[rows,2 * D_RWKV:3 * D_RWKV]
            t["xw"] = lora(jnp.tanh(pa[rows,4 * D_RWKV:4 * D_RWKV + LORA]), wup_ref)
            t["xa"] = lora(pa[rows,4 * D_RWKV + LORA:], aup_ref)
            if first:
                vfo_ref[bi, rows, :] = t["v"]
            else:
                t["xv"] = lora(t["v"], vdn_ref)

        def mix_value():
            if not first:
                nu = _sigmoid(v0_ref[...] + lora(t.pop("xv"), vup_ref))
                t["v"] = t["v"] + (vf_ref[bi, rows, :] - t["v"]) * nu

        def sum_heads():
            r = pa[rows,0:D_RWKV]
            k = pa[rows,D_RWKV:2 * D_RWKV]
            t["lw"] = -math.exp(-0.5) * _sigmoid(w0_ref[...] + t.pop("xw"))
            t["a"] = _sigmoid(a0_ref[...] + t.pop("xa"))
            t["kk"] = k * kk_ref[...]
            t["k2"] = k * (1.0 + (t["a"] - 1.0) * ka_ref[...])
            t["ssq"], t["rk"] = head_sums(t["kk"] * t["kk"], r * t["k2"] * rk_ref[...])

        def cumulate():
            bonus[rows, :] = t.pop("rk") * t["v"]
            t["kn"] = t.pop("kk") * lax.rsqrt(jnp.maximum(t.pop("ssq"), 1e-24))
            lw_hi = t["lw"].astype(BF16)
            lw_lo = (t["lw"] - lw_hi.astype(F32)).astype(BF16)
            t["cum"] = [_dot(tri, lw_hi[j * c:(j + 1) * c]) + _dot(tri, lw_lo[j * c:(j + 1) * c])
                        for j in range(CHUNK_GROUP)]

        def scale():
            r = pa[rows,0:D_RWKV]
            kb = t["kn"] * t["a"]
            vb = t["v"].astype(BF16)
            parts = {key: [] for key in ("ar", "bk", "bkend", "v", "g_end")}
            for j in range(CHUNK_GROUP):
                cr = slice(j * c, (j + 1) * c)
                cum, lw = t["cum"][j], t["lw"][cr]
                tot = cum[c - 1:c, :]
                g_inv = jnp.exp(-cum)
                g_rem = jnp.exp(tot - cum)
                ar = jnp.concatenate([-t["kn"][cr] * jnp.exp(cum - lw), r[cr] * jnp.exp(cum)],
                                     axis=0).astype(BF16)
                bk = jnp.concatenate([kb[cr] * g_inv, t["k2"][cr] * g_inv], axis=0).T.astype(BF16)
                bkend = jnp.concatenate([kb[cr] * g_rem, t["k2"][cr] * g_rem],
                                        axis=0).astype(BF16)
                g_end = jnp.exp(tot)
                parts["ar"] += [ar[:, ln] for ln in lanes]
                parts["bk"] += [bk[ln, :] for ln in lanes]
                parts["bkend"] += [bkend[:, ln] for ln in lanes]
                parts["v"] += [vb[cr][:, ln] for ln in lanes]
                parts["g_end"] += [g_end[:, ln] for ln in lanes]
            st.update(parts)

        return [project, mix_value, sum_heads, cumulate, scale]

    def finish_stages(bi, ig):
        rows = slice(ig * group, (ig + 1) * group)
        pa, bonus = pa_ref.at[bi], bonus_s.at[bi]
        t = {}

        def mean():
            t["o"] = o_s[bi, rows, :]
            t["mean"] = head_sums(t["o"])[0] * (1.0 / HEAD_DIM)

        def variance():
            t["d"] = t.pop("o") - t.pop("mean")
            t["var"] = head_sums(t["d"] * t["d"])[0] * (1.0 / HEAD_DIM)

        def store():
            on = t.pop("d") * lax.rsqrt(t.pop("var") + GN_EPS) * gng_ref[...] + gnb_ref[...]
            gate = pa[rows,3 * D_RWKV:4 * D_RWKV]
            y_ref[bi, rows, :] = ((on + bonus[rows, :]) * gate).astype(y_ref.dtype)

        return [mean, variance, store]

    n_groups = ts // group
    operands = {}
    for it in range(-2, n_groups + 1):
        streams = []
        for bi in range(nb):
            if 0 <= it + 2 < n_groups:
                operands[bi, it + 2] = {}
                streams.append(prepare_stages(bi, it + 2, operands[bi, it + 2]))
            if 0 <= it + 1 < n_groups:
                streams.append(_scan_stages_a(operands[bi, it + 1], strict, incl, eye))
            if 0 <= it < n_groups:
                st = operands.pop((bi, it))
                streams.append(sum((_scan_stages_b(st, j * N_HEADS, s_scr.at[bi], o_s.at[bi],
                                                   it * group + j * c)
                                    for j in range(CHUNK_GROUP)), []))
            if 0 <= it - 1 < n_groups:
                streams.append(finish_stages(bi, it - 1))
        _interleave(streams)


def _rwkv(pa3, vf3, prm, first):
    b, s, _ = pa3.shape
    nb, ts, c = SCAN_SEQS, SCAN_TILE, CHUNK
    row = lambda n: pl.BlockSpec((1, n), lambda i, j: (0, 0))
    mat = lambda m, n: pl.BlockSpec((m, n), lambda i, j: (0, 0))
    seq = lambda n: pl.BlockSpec((nb, ts, n), lambda i, j: (i, j, 0))
    in_specs = [seq(N_A)]
    args = [pa3]
    if not first:
        in_specs.append(seq(D_RWKV))
        args.append(vf3)
    in_specs += [row(D_RWKV), mat(LORA, D_RWKV), row(D_RWKV), mat(LORA, D_RWKV)]
    args += [prm["w0"], prm["w_up"], prm["a0"], prm["a_up"]]
    if not first:
        in_specs += [row(D_RWKV), mat(D_RWKV, MV_LORA), mat(MV_LORA, D_RWKV)]
        args += [prm["v0"], prm["v_down"], prm["v_up"]]
    in_specs += [row(D_RWKV)] * 5 + [mat(LANES, LANES)]
    args += [prm["k_k"], prm["k_a"], prm["r_k"], prm["gn_g"], prm["gn_b"], prm["e"]]
    out_specs = [seq(D_RWKV)]
    out_shape = [jax.ShapeDtypeStruct((b, s, D_RWKV), BF16)]
    if first:
        out_specs.append(seq(D_RWKV))
        out_shape.append(jax.ShapeDtypeStruct((b, s, D_RWKV), F32))
    slab = pltpu.VMEM((nb, ts, D_RWKV), F32)
    scratch = [pltpu.VMEM((nb, N_HEADS, HEAD_DIM, HEAD_DIM), F32), slab, slab]
    return pl.pallas_call(
        functools.partial(_rwkv_kernel, first, nb, ts, c),
        grid=(b // nb, s // ts),
        in_specs=in_specs,
        out_specs=out_specs,
        out_shape=out_shape,
        scratch_shapes=scratch,
        compiler_params=pltpu.CompilerParams(
            dimension_semantics=("parallel", "arbitrary"), vmem_limit_bytes=VMEM_LIMIT),
        name="rwkv_first" if first else "rwkv",
    )(*args)


def _conv_kernel(ts, pc_ref, w_ref, b_ref, g_ref, beta_ref, y_ref, ext, shifted):
    @pl.when(pl.program_id(1) == 0)
    def _():
        ext[0:CONV_HALO, :] = jnp.zeros((CONV_HALO, D_CONV), F32)
        ext[CONV_HALO + ts:, :] = jnp.zeros((SUBLANES, D_CONV), F32)

    ext[CONV_HALO:CONV_HALO + ts, :] = pc_ref[:, 0:D_CONV]
    n_sh = shifted.shape[1]
    for r in range(2, SUBLANES, 2):
        shifted[r // 2 - 1] = ext[r:r + n_sh, :]
    base = CONV_HALO - (CONV_WIDTH - 1)
    acc = jnp.zeros((ts, D_CONV), F32) + b_ref[...]
    late = jnp.zeros((ts + SUBLANES, D_CONV), F32)
    for j in range(CONV_WIDTH):
        q, r = divmod(base + j, SUBLANES)
        src = ext if r < 2 else shifted.at[r // 2 - 1]
        lo = q * SUBLANES
        if r % 2 == 0:
            acc = acc + w_ref[j:j + 1, :] * src[lo:lo + ts, :]
        else:
            late = late + w_ref[j:j + 1, :] * src[lo:lo + ts + SUBLANES, :]
    acc = acc + late[1:ts + 1, :]
    ext[0:CONV_HALO, :] = ext[ts:ts + CONV_HALO, :]
    mean = jnp.mean(acc, axis=-1, keepdims=True)
    d = acc - mean
    var = jnp.mean(d * d, axis=-1, keepdims=True)
    cn = d * lax.rsqrt(var + LN_EPS) * g_ref[...] + beta_ref[...]
    y_ref[...] = ((cn * _sigmoid(cn)) * pc_ref[:, D_CONV:]).astype(y_ref.dtype)


def _conv(pc3, prm):
    b, s, _ = pc3.shape
    ts = CONV_TILE
    row = lambda n: pl.BlockSpec((1, n), lambda i, j: (0, 0))
    return pl.pallas_call(
        functools.partial(_conv_kernel, ts),
        grid=(b, s // ts),
        in_specs=[
            pl.BlockSpec((None, ts, N_C), lambda i, j: (i, j, 0)),
            pl.BlockSpec((CONV_HALO, D_CONV), lambda i, j: (0, 0)),
            row(D_CONV), row(D_CONV), row(D_CONV),
        ],
        out_specs=pl.BlockSpec((None, ts, D_CONV), lambda i, j: (i, j, 0)),
        out_shape=jax.ShapeDtypeStruct((b, s, D_CONV), BF16),
        scratch_shapes=[
            pltpu.VMEM((ts + CONV_HALO + SUBLANES, D_CONV), F32),
            pltpu.VMEM((SUBLANES // 2 - 1, ts + CONV_HALO, D_CONV), F32),
        ],
        compiler_params=pltpu.CompilerParams(
            dimension_semantics=("parallel", "arbitrary"), vmem_limit_bytes=VMEM_LIMIT),
        name="conv",
    )(pc3, prm["dw_w"], prm["dw_b"], prm["ln_g"], prm["ln_b"])


def _outproj_kernel(x_ref, yr_ref, yc_ref, wr_ref, wc_ref, g_ref, o_ref):
    out = _dot(yr_ref[...], wr_ref[...]) + _dot(yc_ref[...], wc_ref[...])
    ms = jnp.mean(out * out, axis=-1, keepdims=True)
    o_ref[...] = x_ref[...] + out * lax.rsqrt(ms + RMS_EPS) * g_ref[...]


def _outproj(x2, yr, yc, wr, wc, g):
    rows = x2.shape[0]
    tm = OUT_TILE
    return pl.pallas_call(
        _outproj_kernel,
        grid=(rows // tm,),
        in_specs=[
            pl.BlockSpec((tm, D_MODEL), lambda i: (i, 0)),
            pl.BlockSpec((tm, D_RWKV), lambda i: (i, 0)),
            pl.BlockSpec((tm, D_CONV), lambda i: (i, 0)),
            pl.BlockSpec((D_RWKV, D_MODEL), lambda i: (0, 0)),
            pl.BlockSpec((D_CONV, D_MODEL), lambda i: (0, 0)),
            pl.BlockSpec((1, D_MODEL), lambda i: (0, 0)),
        ],
        out_specs=pl.BlockSpec((tm, D_MODEL), lambda i: (i, 0)),
        out_shape=jax.ShapeDtypeStruct((rows, D_MODEL), F32),
        compiler_params=pltpu.CompilerParams(
            dimension_semantics=("parallel",), vmem_limit_bytes=VMEM_LIMIT),
        name="outproj",
    )(x2, yr, yc, wr, wc, g)


def kernel(x, pre_g, post_g, w_in, mu_shift, w0, w_up, a0, a_up, v0, v_down, v_up, k_k, k_a, r_k,
           gn_g, gn_b, dw_w, dw_b, ln_g, ln_b, w_out):
    b, s, _ = x.shape
    depth = w_in.shape[0]
    assert s % SCAN_TILE == 0 and s % CONV_TILE == 0 and s % ROW_TILE == 0 and b % SCAN_SEQS == 0
    assert (b * s) % OUT_TILE == 0
    assert SCAN_TILE % (CHUNK * CHUNK_GROUP) == 0 and CHUNK >= 8
    assert CONV_TILE >= CONV_HALO >= CONV_WIDTH - 1
    head = jnp.arange(LANES, dtype=jnp.int32) // HEAD_DIM
    e = (head[:, None] == head[None, :]).astype(BF16)
    w_in_bf = w_in.astype(BF16)
    w_out_bf = w_out.astype(BF16)
    dw_w_pad = jnp.pad(dw_w, ((0, 0), (0, CONV_HALO - CONV_WIDTH), (0, 0)))
    r1 = lambda t: t.reshape(1, -1)

    x2 = x.reshape(b * s, D_MODEL)
    v_first = None
    for l in range(depth):
        pa, pc = _inproj(x2, r1(pre_g[l]), w_in_bf[l, :, :N_A], w_in_bf[l, :, N_A:],
                         r1(mu_shift[l]), s)
        prm = dict(w0=r1(w0[l]), w_up=w_up[l], a0=r1(a0[l]), a_up=a_up[l],
                   k_k=r1(k_k[l]), k_a=r1(k_a[l]), r_k=r1(r_k[l]), gn_g=r1(gn_g[l]),
                   gn_b=r1(gn_b[l]), e=e, dw_w=dw_w_pad[l], dw_b=r1(dw_b[l]), ln_g=r1(ln_g[l]),
                   ln_b=r1(ln_b[l]))
        pa3 = pa.reshape(b, s, N_A)
        if l == 0:
            yr, v_first = _rwkv(pa3, None, prm, True)
        else:
            prm.update(v0=r1(v0[l - 1]), v_down=v_down[l - 1], v_up=v_up[l - 1])
            yr = _rwkv(pa3, v_first, prm, False)[0]
        yc = _conv(pc.reshape(b, s, N_C), prm)
        x2 = _outproj(x2, yr.reshape(b * s, D_RWKV), yc.reshape(b * s, D_CONV),
                      w_out_bf[l, :D_RWKV], w_out_bf[l, D_RWKV:], r1(post_g[l]))
    return x2.reshape(b, s, D_MODEL)
```

```python
import functools
import math

import jax
import jax.numpy as jnp
from jax import lax
from jax.experimental import pallas as pl
from jax.experimental.pallas import tpu as pltpu

D_MODEL = 1024
D_RWKV = 512
HEAD_DIM = 64
N_HEADS = D_RWKV // HEAD_DIM
D_CONV = 512
CONV_WIDTH = 31
LORA = 64
MV_LORA = 32
N_A = 4 * D_RWKV + 2 * LORA
N_B = 3 * D_CONV
N_C = 2 * D_CONV
RMS_EPS = 1e-6
GN_EPS = 64e-5
LN_EPS = 1e-5

SUBLANES = 8
LANES = 128
CHUNK = 64
CHUNK_GROUP = 2
ROW_TILE = 512
OUT_TILE = 1024
SCAN_TILE = 1024
SCAN_SEQS = 1
CONV_TILE = 256
CONV_HALO = 32
VMEM_LIMIT = 48 * 1024 * 1024

F32 = jnp.float32
BF16 = jnp.bfloat16


def _dot(a, b, dims=((1,), (0,))):
    return lax.dot_general(a, b, (dims, ((), ())), preferred_element_type=F32)


def _sigmoid(x):
    return 1.0 / (1.0 + jnp.exp(-x))


def _inproj_kernel(tiles_per_seq, x_ref, g_ref, wa_ref, wb_ref, mu_ref, pa_ref, pc_ref, carry):
    tm = x_ref.shape[0]

    @pl.when(pl.program_id(0) % tiles_per_seq == 0)
    def _():
        carry[...] = jnp.zeros_like(carry)

    x = x_ref[...]
    ms = jnp.mean(x * x, axis=-1, keepdims=True)
    h = (x * lax.rsqrt(ms + RMS_EPS) * g_ref[...]).astype(BF16)
    pa = _dot(h, wa_ref[...])
    row = lax.broadcasted_iota(jnp.int32, (tm, 1), 0)
    prev = jnp.where(row == 0, carry[0:1, :], pltpu.roll(pa, 1, 0))
    carry[0:1, :] = pa[tm - 1:tm, :]
    ps = pa + (prev - pa) * mu_ref[...]
    z = ps[:, 3 * D_RWKV:4 * D_RWKV]
    pa_ref[:, 0:3 * D_RWKV] = ps[:, 0:3 * D_RWKV]
    pa_ref[:, 3 * D_RWKV:4 * D_RWKV] = z * _sigmoid(z)
    pa_ref[:, 4 * D_RWKV:] = ps[:, 4 * D_RWKV:]
    pb = _dot(h, wb_ref[...])
    zc = pb[:, 2 * D_CONV:]
    pc_ref[:, 0:D_CONV] = pb[:, 0:D_CONV] * _sigmoid(pb[:, D_CONV:2 * D_CONV])
    pc_ref[:, D_CONV:] = zc * _sigmoid(zc)


def _inproj(x2, g, wa, wb, mu, seq_len):
    rows = x2.shape[0]
    tm = ROW_TILE
    return pl.pallas_call(
        functools.partial(_inproj_kernel, seq_len // tm),
        grid=(rows // tm,),
        in_specs=[
            pl.BlockSpec((tm, D_MODEL), lambda i: (i, 0)),
            pl.BlockSpec((1, D_MODEL), lambda i: (0, 0)),
            pl.BlockSpec((D_MODEL, N_A), lambda i: (0, 0)),
            pl.BlockSpec((D_MODEL, N_B), lambda i: (0, 0)),
            pl.BlockSpec((1, N_A), lambda i: (0, 0)),
        ],
        out_specs=[
            pl.BlockSpec((tm, N_A), lambda i: (i, 0)),
            pl.BlockSpec((tm, N_C), lambda i: (i, 0)),
        ],
        out_shape=[
            jax.ShapeDtypeStruct((rows, N_A), F32),
            jax.ShapeDtypeStruct((rows, N_C), F32),
        ],
        scratch_shapes=[pltpu.VMEM((SUBLANES, N_A), F32)],
        compiler_params=pltpu.CompilerParams(
            dimension_semantics=("arbitrary",), vmem_limit_bytes=VMEM_LIMIT),
        name="inproj",
    )(x2, g, wa, wb, mu)


def _interleave(streams):
    tagged = []
    for s, stages in enumerate(streams):
        tagged += [((k + 0.5) / len(stages), s, stage) for k, stage in enumerate(stages)]
    for _, _, stage in sorted(tagged, key=lambda entry: entry[:2]):
        stage()


def _scan_stages_a(st, strict, incl, eye):
    heads = range(len(st["v"]))
    c = st["v"][0].shape[0]

    def gram():
        st["m"] = [_dot(st["ar"][h], st["bk"][h]) for h in heads]

    def square():
        ri = lax.broadcasted_iota(jnp.int32, (2 * c, 2 * c), 0)
        ci = lax.broadcasted_iota(jnp.int32, (2 * c, 2 * c), 1)
        row_t, col_s = ri & (c - 1), ci & (c - 1)
        keep = (row_t > col_s) | ((ri >= c) & (row_t == col_s))
        m = [jnp.where(keep, mh, 0.0) for mh in st.pop("m")]
        a_ab = [m[h][:c, :c] for h in heads]
        a_k = [m[h][:, c:].astype(BF16) for h in heads]
        st["a_rb"] = [m[h][c:, :c].astype(BF16) for h in heads]
        p = [a_ab[h].astype(BF16) for h in heads]
        st["t"] = [eye + a_ab[h] for h in heads]
        st["p"] = [_dot(p[h], p[h]).astype(BF16) for h in heads]
        st["akv"] = [_dot(a_k[h], st["v"][h]) for h in heads]

    def double():
        p, t = st["p"], st["t"]
        pt = [_dot(jnp.concatenate([p[h], t[h].astype(BF16)], axis=0), p[h]) for h in heads]
        st["t"] = [t[h] + pt[h][c:] for h in heads]
        st["p"] = [pt[h][:c].astype(BF16) for h in heads]

    def last():
        p, t = st.pop("p"), st["t"]
        st["t"] = [(t[h] + _dot(t[h].astype(BF16), p[h])).astype(BF16) for h in heads]

    def apply_t():
        t, akv = st.pop("t"), st.pop("akv")
        st["at2"] = [_dot(t[h], st["ar"][h][:c]).astype(BF16) for h in heads]
        st["ua"] = [_dot(t[h], akv[h][:c].astype(BF16)) for h in heads]
        st["yv"] = [akv[h][c:] for h in heads]

    return [gram, square] + [double] * (c.bit_length() - 3) + [last, apply_t]


def _scan_stages_b(st, lo, s_scr, o_s, row0):
    heads = range(s_scr.shape[0])
    c = st["v"][0].shape[0]
    s0, z = {}, {}

    def read_state():
        for h in heads:
            s0[h] = s_scr[h]
            lhs = jnp.concatenate([st["at2"][lo + h], st["ar"][lo + h][c:]], axis=0)
            z[h] = _dot(lhs, s0[h].astype(BF16), ((1,), (1,)))

    def write_state():
        u = [(z[h][:c] + st["ua"][lo + h]).astype(BF16) for h in heads]
        y = [z[h][c:] + st["yv"][lo + h] + _dot(st["a_rb"][lo + h], u[h]) for h in heads]
        for h in heads:
            uv = jnp.concatenate([u[h], st["v"][lo + h]], axis=0)
            s_scr[h] = s0[h] * st["g_end"][lo + h] + _dot(uv, st["bkend"][lo + h], ((0,), (0,)))
            o_s[row0:row0 + c, h * HEAD_DIM:(h + 1) * HEAD_DIM] = y[h]

    return [read_state, write_state]


def _rwkv_kernel(first, nb, ts, c, *refs):
    refs = list(refs)
    pa_ref = refs.pop(0)
    vf_ref = None if first else refs.pop(0)
    w0_ref, wup_ref, a0_ref, aup_ref = (refs.pop(0) for _ in range(4))
    if not first:
        v0_ref, vdn_ref, vup_ref = (refs.pop(0) for _ in range(3))
    kk_ref, ka_ref, rk_ref, gng_ref, gnb_ref, e_ref = (refs.pop(0) for _ in range(6))
    y_ref = refs.pop(0)
    vfo_ref = refs.pop(0) if first else None
    s_scr, o_s, bonus_s = refs

    @pl.when(pl.program_id(1) == 0)
    def _():
        s_scr[...] = jnp.zeros_like(s_scr)

    e = e_ref[...]

    def head_sums(*terms):
        n = terms[0].shape[0]
        tiles = [t[:, lo:lo + LANES].astype(BF16) for t in terms for lo in range(0, D_RWKV, LANES)]
        sums = _dot(jnp.concatenate(tiles, axis=0), e)
        per = D_RWKV // LANES
        return [jnp.concatenate([sums[(i * per + j) * n:(i * per + j + 1) * n] for j in range(per)],
                                axis=1) for i in range(len(terms))]
    lora = lambda t, w_ref: _dot(t.astype(BF16), w_ref[...].astype(BF16))
    ri = lax.broadcasted_iota(jnp.int32, (c, c), 0)
    ci = lax.broadcasted_iota(jnp.int32, (c, c), 1)
    strict = ri > ci
    incl = ri >= ci
    eye = (ri == ci).astype(F32)
    tri = incl.astype(BF16)
    lanes = [slice(h * HEAD_DIM, (h + 1) * HEAD_DIM) for h in range(N_HEADS)]
    group = c * CHUNK_GROUP

    def prepare_stages(bi, ig, st):
        rows = slice(ig * group, (ig + 1) * group)
        pa, bonus = pa_ref.at[bi], bonus_s.at[bi]
        t = {}

        def project():
            t["v"] = pa[rows,2 * D_RWKV:3 * D_RWKV]
            t["xw"] = lora(jnp.tanh(pa[rows,4 * D_RWKV:4 * D_RWKV + LORA]), wup_ref)
            t["xa"] = lora(pa[rows,4 * D_RWKV + LORA:], aup_ref)
            if first:
                vfo_ref[bi, rows, :] = t["v"]
            else:
                t["xv"] = lora(t["v"], vdn_ref)

        def mix_value():
            if not first:
                nu = _sigmoid(v0_ref[...] + lora(t.pop("xv"), vup_ref))
                t["v"] = t["v"] + (vf_ref[bi, rows, :] - t["v"]) * nu

        def sum_heads():
            r = pa[rows,0:D_RWKV]
            k = pa[rows,D_RWKV:2 * D_RWKV]
            t["lw"] = -math.exp(-0.5) * _sigmoid(w0_ref[...] + t.pop("xw"))
            t["a"] = _sigmoid(a0_ref[...] + t.pop("xa"))
            t["kk"] = k * kk_ref[...]
            t["k2"] = k * (1.0 + (t["a"] - 1.0) * ka_ref[...])
            t["ssq"], t["rk"] = head_sums(t["kk"] * t["kk"], r * t["k2"] * rk_ref[...])

        def cumulate():
            bonus[rows, :] = t.pop("rk") * t["v"]
            t["kn"] = t.pop("kk") * lax.rsqrt(jnp.maximum(t.pop("ssq"), 1e-24))
            lw_hi = t["lw"].astype(BF16)
            lw_lo = (t["lw"] - lw_hi.astype(F32)).astype(BF16)
            t["cum"] = [_dot(tri, lw_hi[j * c:(j + 1) * c]) + _dot(tri, lw_lo[j * c:(j + 1) * c])
                        for j in range(CHUNK_GROUP)]

        def scale():
            r = pa[rows,0:D_RWKV]
            kb = t["kn"] * t["a"]
            vb = t["v"].astype(BF16)
            parts = {key: [] for key in ("ar", "bk", "bkend", "v", "g_end")}
            for j in range(CHUNK_GROUP):
                cr = slice(j * c, (j + 1) * c)
                cum, lw = t["cum"][j], t["lw"][cr]
                tot = cum[c - 1:c, :]
                g_inv = jnp.exp(-cum)
                g_rem = jnp.exp(tot - cum)
                ar = jnp.concatenate([-t["kn"][cr] * jnp.exp(cum - lw), r[cr] * jnp.exp(cum)],
                                     axis=0).astype(BF16)
                bk = jnp.concatenate([kb[cr] * g_inv, t["k2"][cr] * g_inv], axis=0).T.astype(BF16)
                bkend = jnp.concatenate([kb[cr] * g_rem, t["k2"][cr] * g_rem],
                                        axis=0).astype(BF16)
                g_end = jnp.exp(tot)
                parts["ar"] += [ar[:, ln] for ln in lanes]
                parts["bk"] += [bk[ln, :] for ln in lanes]
                parts["bkend"] += [bkend[:, ln] for ln in lanes]
                parts["v"] += [vb[cr][:, ln] for ln in lanes]
                parts["g_end"] += [g_end[:, ln] for ln in lanes]
            st.update(parts)

        return [project, mix_value, sum_heads, cumulate, scale]

    def finish_stages(bi, ig):
        rows = slice(ig * group, (ig + 1) * group)
        pa, bonus = pa_ref.at[bi], bonus_s.at[bi]
        t = {}

        def mean():
            t["o"] = o_s[bi, rows, :]
            t["mean"] = head_sums(t["o"])[0] * (1.0 / HEAD_DIM)

        def variance():
            t["d"] = t.pop("o") - t.pop("mean")
            t["var"] = head_sums(t["d"] * t["d"])[0] * (1.0 / HEAD_DIM)

        def store():
            on = t.pop("d") * lax.rsqrt(t.pop("var") + GN_EPS) * gng_ref[...] + gnb_ref[...]
            gate = pa[rows,3 * D_RWKV:4 * D_RWKV]
            y_ref[bi, rows, :] = ((on + bonus[rows, :]) * gate).astype(y_ref.dtype)

        return [mean, variance, store]

    n_groups = ts // group
    operands = {}
    for it in range(-2, n_groups + 1):
        streams = []
        for bi in range(nb):
            if 0 <= it + 2 < n_groups:
                operands[bi, it + 2] = {}
                streams.append(prepare_stages(bi, it + 2, operands[bi, it + 2]))
            if 0 <= it + 1 < n_groups:
                streams.append(_scan_stages_a(operands[bi, it + 1], strict, incl, eye))
            if 0 <= it < n_groups:
                st = operands.pop((bi, it))
                streams.append(sum((_scan_stages_b(st, j * N_HEADS, s_scr.at[bi], o_s.at[bi],
                                                   it * group + j * c)
                                    for j in range(CHUNK_GROUP)), []))
            if 0 <= it - 1 < n_groups:
                streams.append(finish_stages(bi, it - 1))
        _interleave(streams)


def _rwkv(pa3, vf3, prm, first):
    b, s, _ = pa3.shape
    nb, ts, c = SCAN_SEQS, SCAN_TILE, CHUNK
    row = lambda n: pl.BlockSpec((1, n), lambda i, j: (0, 0))
    mat = lambda m, n: pl.BlockSpec((m, n), lambda i, j: (0, 0))
    seq = lambda n: pl.BlockSpec((nb, ts, n), lambda i, j: (i, j, 0))
    in_specs = [seq(N_A)]
    args = [pa3]
    if not first:
        in_specs.append(seq(D_RWKV))
        args.append(vf3)
    in_specs += [row(D_RWKV), mat(LORA, D_RWKV), row(D_RWKV), mat(LORA, D_RWKV)]
    args += [prm["w0"], prm["w_up"], prm["a0"], prm["a_up"]]
    if not first:
        in_specs += [row(D_RWKV), mat(D_RWKV, MV_LORA), mat(MV_LORA, D_RWKV)]
        args += [prm["v0"], prm["v_down"], prm["v_up"]]
    in_specs += [row(D_RWKV)] * 5 + [mat(LANES, LANES)]
    args += [prm["k_k"], prm["k_a"], prm["r_k"], prm["gn_g"], prm["gn_b"], prm["e"]]
    out_specs = [seq(D_RWKV)]
    out_shape = [jax.ShapeDtypeStruct((b, s, D_RWKV), BF16)]
    if first:
        out_specs.append(seq(D_RWKV))
        out_shape.append(jax.ShapeDtypeStruct((b, s, D_RWKV), F32))
    slab = pltpu.VMEM((nb, ts, D_RWKV), F32)
    scratch = [pltpu.VMEM((nb, N_HEADS, HEAD_DIM, HEAD_DIM), F32), slab, slab]
    return pl.pallas_call(
        functools.partial(_rwkv_kernel, first, nb, ts, c),
        grid=(b // nb, s // ts),
        in_specs=in_specs,
        out_specs=out_specs,
        out_shape=out_shape,
        scratch_shapes=scratch,
        compiler_params=pltpu.CompilerParams(
            dimension_semantics=("parallel", "arbitrary"), vmem_limit_bytes=VMEM_LIMIT),
        name="rwkv_first" if first else "rwkv",
    )(*args)


def _conv_kernel(ts, pc_ref, w_ref, b_ref, g_ref, beta_ref, y_ref, ext, shifted):
    @pl.when(pl.program_id(1) == 0)
    def _():
        ext[0:CONV_HALO, :] = jnp.zeros((CONV_HALO, D_CONV), F32)
        ext[CONV_HALO + ts:, :] = jnp.zeros((SUBLANES, D_CONV), F32)

    ext[CONV_HALO:CONV_HALO + ts, :] = pc_ref[:, 0:D_CONV]
    n_sh = shifted.shape[1]
    for r in range(2, SUBLANES, 2):
        shifted[r // 2 - 1] = ext[r:r + n_sh, :]
    base = CONV_HALO - (CONV_WIDTH - 1)
    acc = jnp.zeros((ts, D_CONV), F32) + b_ref[...]
    late = jnp.zeros((ts + SUBLANES, D_CONV), F32)
    for j in range(CONV_WIDTH):
        q, r = divmod(base + j, SUBLANES)
        src = ext if r < 2 else shifted.at[r // 2 - 1]
        lo = q * SUBLANES
        if r % 2 == 0:
            acc = acc + w_ref[j:j + 1, :] * src[lo:lo + ts, :]
        else:
            late = late + w_ref[j:j + 1, :] * src[lo:lo + ts + SUBLANES, :]
    acc = acc + late[1:ts + 1, :]
    ext[0:CONV_HALO, :] = ext[ts:ts + CONV_HALO, :]
    mean = jnp.mean(acc, axis=-1, keepdims=True)
    d = acc - mean
    var = jnp.mean(d * d, axis=-1, keepdims=True)
    cn = d * lax.rsqrt(var + LN_EPS) * g_ref[...] + beta_ref[...]
    y_ref[...] = ((cn * _sigmoid(cn)) * pc_ref[:, D_CONV:]).astype(y_ref.dtype)


def _conv(pc3, prm):
    b, s, _ = pc3.shape
    ts = CONV_TILE
    row = lambda n: pl.BlockSpec((1, n), lambda i, j: (0, 0))
    return pl.pallas_call(
        functools.partial(_conv_kernel, ts),
        grid=(b, s // ts),
        in_specs=[
            pl.BlockSpec((None, ts, N_C), lambda i, j: (i, j, 0)),
            pl.BlockSpec((CONV_HALO, D_CONV), lambda i, j: (0, 0)),
            row(D_CONV), row(D_CONV), row(D_CONV),
        ],
        out_specs=pl.BlockSpec((None, ts, D_CONV), lambda i, j: (i, j, 0)),
        out_shape=jax.ShapeDtypeStruct((b, s, D_CONV), BF16),
        scratch_shapes=[
            pltpu.VMEM((ts + CONV_HALO + SUBLANES, D_CONV), F32),
            pltpu.VMEM((SUBLANES // 2 - 1, ts + CONV_HALO, D_CONV), F32),
        ],
        compiler_params=pltpu.CompilerParams(
            dimension_semantics=("parallel", "arbitrary"), vmem_limit_bytes=VMEM_LIMIT),
        name="conv",
    )(pc3, prm["dw_w"], prm["dw_b"], prm["ln_g"], prm["ln_b"])


def _outproj_kernel(x_ref, yr_ref, yc_ref, wr_ref, wc_ref, g_ref, o_ref):
    out = _dot(yr_ref[...], wr_ref[...]) + _dot(yc_ref[...], wc_ref[...])
    ms = jnp.mean(out * out, axis=-1, keepdims=True)
    o_ref[...] = x_ref[...] + out * lax.rsqrt(ms + RMS_EPS) * g_ref[...]


def _outproj(x2, yr, yc, wr, wc, g):
    rows = x2.shape[0]
    tm = OUT_TILE
    return pl.pallas_call(
        _outproj_kernel,
        grid=(rows // tm,),
        in_specs=[
            pl.BlockSpec((tm, D_MODEL), lambda i: (i, 0)),
            pl.BlockSpec((tm, D_RWKV), lambda i: (i, 0)),
            pl.BlockSpec((tm, D_CONV), lambda i: (i, 0)),
            pl.BlockSpec((D_RWKV, D_MODEL), lambda i: (0, 0)),
            pl.BlockSpec((D_CONV, D_MODEL), lambda i: (0, 0)),
            pl.BlockSpec((1, D_MODEL), lambda i: (0, 0)),
        ],
        out_specs=pl.BlockSpec((tm, D_MODEL), lambda i: (i, 0)),
        out_shape=jax.ShapeDtypeStruct((rows, D_MODEL), F32),
        compiler_params=pltpu.CompilerParams(
            dimension_semantics=("parallel",), vmem_limit_bytes=VMEM_LIMIT),
        name="outproj",
    )(x2, yr, yc, wr, wc, g)


def kernel(x, pre_g, post_g, w_in, mu_shift, w0, w_up, a0, a_up, v0, v_down, v_up, k_k, k_a, r_k,
           gn_g, gn_b, dw_w, dw_b, ln_g, ln_b, w_out):
    b, s, _ = x.shape
    depth = w_in.shape[0]
    assert s % SCAN_TILE == 0 and s % CONV_TILE == 0 and s % ROW_TILE == 0 and b % SCAN_SEQS == 0
    assert (b * s) % OUT_TILE == 0
    assert SCAN_TILE % (CHUNK * CHUNK_GROUP) == 0 and CHUNK >= 8
    assert CONV_TILE >= CONV_HALO >= CONV_WIDTH - 1
    head = jnp.arange(LANES, dtype=jnp.int32) // HEAD_DIM
    e = (head[:, None] == head[None, :]).astype(BF16)
    w_in_bf = w_in.astype(BF16)
    w_out_bf = w_out.astype(BF16)
    dw_w_pad = jnp.pad(dw_w, ((0, 0), (0, CONV_HALO - CONV_WIDTH), (0, 0)))
    r1 = lambda t: t.reshape(1, -1)

    x2 = x.reshape(b * s, D_MODEL)
    v_first = None
    for l in range(depth):
        pa, pc = _inproj(x2, r1(pre_g[l]), w_in_bf[l, :, :N_A], w_in_bf[l, :, N_A:],
                         r1(mu_shift[l]), s)
        prm = dict(w0=r1(w0[l]), w_up=w_up[l], a0=r1(a0[l]), a_up=a_up[l],
                   k_k=r1(k_k[l]), k_a=r1(k_a[l]), r_k=r1(r_k[l]), gn_g=r1(gn_g[l]),
                   gn_b=r1(gn_b[l]), e=e, dw_w=dw_w_pad[l], dw_b=r1(dw_b[l]), ln_g=r1(ln_g[l]),
                   ln_b=r1(ln_b[l]))
        pa3 = pa.reshape(b, s, N_A)
        if l == 0:
            yr, v_first = _rwkv(pa3, None, prm, True)
        else:
            prm.update(v0=r1(v0[l - 1]), v_down=v_down[l - 1], v_up=v_up[l - 1])
            yr = _rwkv(pa3, v_first, prm, False)[0]
        yc = _conv(pc.reshape(b, s, N_C), prm)
        x2 = _outproj(x2, yr.reshape(b * s, D_RWKV), yc.reshape(b * s, D_CONV),
                      w_out_bf[l, :D_RWKV], w_out_bf[l, D_RWKV:], r1(post_g[l]))
    return x2.reshape(b, s, D_MODEL)
```
